```python
import math
import jax
import jax.numpy as jnp
from jax import lax
import numpy as np

D_MODEL = 1024
BATCH = 8
SEQ = 4096
DEPTH = 4

N_MEM = 256
N_MIXERS = 3
N_SUBLAYERS = 4
CHUNK = 64
CONV_W = 4
RMS_EPS = 1e-6
MACARON_W = 0.5
D_FF = 2816

DN_QK_HEADS = 8
DN_V_HEADS = 16
DN_DK = 128
DN_DV = 128
DN_Q = DN_QK_HEADS * DN_DK
DN_VAL = DN_V_HEADS * DN_DV
DN_CONV_DIM = 2 * DN_Q + DN_VAL
DN_IN = DN_CONV_DIM + DN_VAL + 2 * DN_V_HEADS

SSD_INNER = 2 * D_MODEL
SSD_HEADDIM = 64
SSD_HEADS = SSD_INNER // SSD_HEADDIM
SSD_GROUPS = 8
SSD_HPG = SSD_HEADS // SSD_GROUPS
SSD_STATE = 128
SSD_BC = SSD_GROUPS * SSD_STATE
SSD_CONV_DIM = SSD_INNER + 2 * SSD_BC
SSD_IN = SSD_INNER + SSD_CONV_DIM + SSD_HEADS

RW_HEAD = 64
RW_HEADS = D_MODEL // RW_HEAD
RW_DECAY_LORA = 64
RW_A_LORA = 64
RW_GATE_LORA = 160
RW_GN_EPS = 64e-5
N_LERP = 6

XA_HEADS = 4
XA_DH = D_MODEL // XA_HEADS

N_DN = (DEPTH + 2) // N_MIXERS
N_SSD = (DEPTH + 1) // N_MIXERS
N_RW = DEPTH // N_MIXERS

kernel_name = 'hybrid_deltanet_ssd_rwkv7_macaron_trunk'

F32 = jnp.float32


def rms_norm(x, g, eps=RMS_EPS):
    xf = x.astype(F32)
    y = xf * lax.rsqrt(jnp.mean(xf * xf, axis=-1, keepdims=True) + eps)
    return (y * g.astype(F32)).astype(x.dtype)


def l2_normalize(x, eps=1e-6):
    xf = x.astype(F32)
    return xf * lax.rsqrt(jnp.sum(xf * xf, axis=-1, keepdims=True) + eps)


def swiglu(x, w_in, w_out):
    gate, up = jnp.split(x @ w_in, 2, axis=-1)
    return (jax.nn.silu(gate) * up) @ w_out


def causal_depthwise_conv(x, w):
    c = x.shape[-1]
    return lax.conv_general_dilated(
        x, w[:, None, :], window_strides=(1,), padding=[(w.shape[0] - 1, 0)],
        dimension_numbers=('NWC', 'WIO', 'NWC'), feature_group_count=c)


def to_chunks(t):
    b, s = t.shape[:2]
    return jnp.swapaxes(t.astype(F32).reshape(b, s // CHUNK, CHUNK, *t.shape[2:]), 0, 1)


def from_chunks(t):
    t = jnp.swapaxes(t, 0, 1)
    return t.reshape(t.shape[0], t.shape[1] * t.shape[2], *t.shape[3:])


def gated_delta_rule_chunked(q, k, v, beta, g):
    b, _, h, dk = k.shape
    dv = v.shape[-1]
    incl = jnp.tril(jnp.ones((CHUNK, CHUNK), dtype=bool))
    strict = jnp.tril(jnp.ones((CHUNK, CHUNK), dtype=bool), k=-1)
    eye = jnp.eye(CHUNK, dtype=F32)

    def step(state, inp):
        qc, kc, vc, bc, gc = inp
        G = jnp.cumsum(gc, axis=1)
        Gh = jnp.swapaxes(G, 1, 2)
        decay = jnp.exp(jnp.where(incl, Gh[..., :, None] - Gh[..., None, :], -jnp.inf))
        kb = kc * bc[..., None]
        a_mat = jnp.where(strict, jnp.einsum('bihk,bjhk->bhij', kb, kc) * decay, 0.0)
        rhs = jnp.concatenate([vc * bc[..., None], kb * jnp.exp(G)[..., None]], axis=-1)
        rhs = jnp.swapaxes(rhs, 1, 2)
        sol = lax.linalg.triangular_solve(eye + a_mat, rhs, left_side=True, lower=True,
                                          unit_diagonal=True)
        u, wk = sol[..., :dv], sol[..., dv:]
        v_new = u - jnp.einsum('bhik,bhkv->bhiv', wk, state)
        qk = jnp.einsum('bihk,bjhk->bhij', qc, kc) * decay
        o = (jnp.einsum('bihk,bhkv->bihv', qc * jnp.exp(G)[..., None], state)
             + jnp.einsum('bhij,bhjv->bihv', qk, v_new))
        k_end = kc * jnp.exp(G[:, -1:, :] - G)[..., None]
        state = (state * jnp.exp(Gh[..., -1])[..., None, None]
                 + jnp.einsum('bihk,bhiv->bhkv', k_end, v_new))
        return state, o

    state0 = jnp.zeros((b, h, dk, dv), F32)
    _, o = lax.scan(step, state0, (to_chunks(q), to_chunks(k), to_chunks(v),
                                   to_chunks(beta), to_chunks(g)))
    return from_chunks(o)


def gated_deltanet_mixer(x, w_in, conv_w, a_log, dt_bias, norm_g, w_out):
    b, s, _ = x.shape
    proj = x @ w_in
    qkv = jax.nn.silu(causal_depthwise_conv(proj[..., :DN_CONV_DIM], conv_w))
    z = proj[..., DN_CONV_DIM:DN_CONV_DIM + DN_VAL]
    beta_raw = proj[..., DN_CONV_DIM + DN_VAL:DN_CONV_DIM + DN_VAL + DN_V_HEADS]
    a_raw = proj[..., DN_CONV_DIM + DN_VAL + DN_V_HEADS:]
    rep = DN_V_HEADS // DN_QK_HEADS
    q = l2_normalize(qkv[..., :DN_Q].reshape(b, s, DN_QK_HEADS, DN_DK)) * DN_DK ** -0.5
    k = l2_normalize(qkv[..., DN_Q:2 * DN_Q].reshape(b, s, DN_QK_HEADS, DN_DK))
    q = jnp.repeat(q, rep, axis=2)
    k = jnp.repeat(k, rep, axis=2)
    v = qkv[..., 2 * DN_Q:].reshape(b, s, DN_V_HEADS, DN_DV)
    beta = jax.nn.sigmoid(beta_raw.astype(F32))
    g = -jnp.exp(a_log.astype(F32)) * jax.nn.softplus(a_raw.astype(F32) + dt_bias.astype(F32))
    o = gated_delta_rule_chunked(q, k, v, beta, g)
    o = rms_norm(o, norm_g) * jax.nn.silu(z.astype(F32).reshape(b, s, DN_V_HEADS, DN_DV))
    return o.reshape(b, s, DN_VAL).astype(x.dtype) @ w_out


def ssd_chunked(xdt, a_dt, bm, cm):
    b, _, gr, e, p = xdt.shape
    n = bm.shape[-1]
    incl = jnp.tril(jnp.ones((CHUNK, CHUNK), dtype=bool))[:, :, None, None]

    def step(state, inp):
        xc, ac, bc, cc = inp
        acum = jnp.cumsum(ac, axis=1)
        seg = jnp.exp(jnp.where(incl, acum[:, :, None] - acum[:, None, :], -jnp.inf))
        cb = jnp.einsum('blgn,bsgn->blsg', cc, bc)
        y = (jnp.einsum('blsg,blsge,bsgep->blgep', cb, seg, xc)
             + jnp.einsum('blgn,bgepn,blge->blgep', cc, state, jnp.exp(acum)))
        to_end = jnp.exp(acum[:, -1:] - acum)
        state = (state * jnp.exp(acum[:, -1])[..., None, None]
                 + jnp.einsum('bsgn,bsge,bsgep->bgepn', bc, to_end, xc))
        return state, y

    state0 = jnp.zeros((b, gr, e, p, n), F32)
    _, y = lax.scan(step, state0, (to_chunks(xdt), to_chunks(a_dt), to_chunks(bm), to_chunks(cm)))
    return from_chunks(y)


def mamba2_mixer(x, w_in, conv_w, conv_b, a_log, dt_bias, d_skip, norm_g, w_out):
    b, s, _ = x.shape
    proj = x @ w_in
    z = proj[..., :SSD_INNER]
    xbc = jax.nn.silu(causal_depthwise_conv(proj[..., SSD_INNER:SSD_INNER + SSD_CONV_DIM], conv_w)
                      + conv_b)
    dt_raw = proj[..., SSD_INNER + SSD_CONV_DIM:]
    xs = xbc[..., :SSD_INNER].astype(F32).reshape(b, s, SSD_GROUPS, SSD_HPG, SSD_HEADDIM)
    bm = xbc[..., SSD_INNER:SSD_INNER + SSD_BC].reshape(b, s, SSD_GROUPS, SSD_STATE)
    cm = xbc[..., SSD_INNER + SSD_BC:].reshape(b, s, SSD_GROUPS, SSD_STATE)
    dt = jax.nn.softplus(dt_raw.astype(F32) + dt_bias.astype(F32)).reshape(b, s, SSD_GROUPS, SSD_HPG)
    a = -jnp.exp(a_log.astype(F32)).reshape(SSD_GROUPS, SSD_HPG)
    y = ssd_chunked(xs * dt[..., None], dt * a, bm, cm)
    y = y + xs * d_skip.astype(F32).reshape(SSD_GROUPS, SSD_HPG, 1)
    gsz = SSD_INNER // SSD_GROUPS
    yz = y.reshape(b, s, SSD_GROUPS, gsz) * jax.nn.silu(z.astype(F32)).reshape(b, s, SSD_GROUPS, gsz)
    yz = rms_norm(yz, norm_g.reshape(SSD_GROUPS, gsz))
    return yz.reshape(b, s, SSD_INNER).astype(x.dtype) @ w_out


def wkv7_scan(r, w, k, v, a, bb):
    b, _, h, n = r.shape

    def step(state, inp):
        rt, wt, kt, vt, at, bt = inp
        sa = jnp.einsum('bhvk,bhk->bhv', state, at)
        state = (state * wt[:, :, None, :] + sa[..., None] * bt[:, :, None, :]
                 + vt[..., None] * kt[:, :, None, :])
        return state, jnp.einsum('bhvk,bhk->bhv', state, rt)

    tm = lambda t: jnp.swapaxes(t, 0, 1)
    _, y = lax.scan(step, jnp.zeros((b, h, n, n), F32),
                    (tm(r), tm(w), tm(k), tm(v), tm(a), tm(bb)))
    return tm(y)


def rwkv7_mixer(x, mu, w_rkv, w0, w1, w2, a0, a1, a2, g1, g2, k_k, k_a, r_k, ln_g, ln_b, w_out):
    b, s, d = x.shape
    xx = jnp.pad(x, ((0, 0), (1, 0), (0, 0)))[:, :-1] - x
    xr, xw, xk, xv, xa, xg = [x + xx * mu[i] for i in range(N_LERP)]
    rkv = jnp.einsum('ibsd,ide->ibse', jnp.stack([xr, xk, xv]), w_rkv)
    r, k, v = rkv[0], rkv[1], rkv[2]
    w = -jax.nn.softplus(-(w0 + jnp.tanh(xw @ w1) @ w2)) - 0.5
    a = jax.nn.sigmoid(a0 + (xa @ a1) @ a2)
    gate = jax.nn.sigmoid(xg @ g1) @ g2
    heads = lambda t: t.astype(F32).reshape(b, s, RW_HEADS, RW_HEAD)
    kk = l2_normalize(heads(k * k_k))
    k = k * (1.0 + (a - 1.0) * k_a)
    rh, kh, vh, ah = heads(r), heads(k), heads(v), heads(a)
    decay = jnp.exp(-jnp.exp(heads(w)))
    y = wkv7_scan(rh, decay, kh, vh, -kk, kk * ah)
    mean = jnp.mean(y, axis=-1, keepdims=True)
    var = jnp.mean(jnp.square(y - mean), axis=-1, keepdims=True)
    y = ((y - mean) * lax.rsqrt(var + RW_GN_EPS)).reshape(b, s, d) * ln_g.astype(F32) + ln_b.astype(F32)
    y = y + (jnp.sum(rh * kh * r_k.astype(F32), axis=-1, keepdims=True) * vh).reshape(b, s, d)
    return (y * gate.astype(F32)).astype(x.dtype) @ w_out


def memory_cross_attention(x, mem_n, w_q, w_kv, w_o):
    b, s, d = x.shape
    m = mem_n.shape[1]
    q = (x @ w_q).reshape(b, s, XA_HEADS, XA_DH)
    k, v = jnp.split(mem_n @ w_kv, 2, axis=-1)
    k = k.reshape(b, m, XA_HEADS, XA_DH)
    v = v.reshape(b, m, XA_HEADS, XA_DH)
    scores = jnp.einsum('bshd,bmhd->bhsm', q, k).astype(F32) * XA_DH ** -0.5
    p = jax.nn.softmax(scores, axis=-1).astype(v.dtype)
    o = jnp.einsum('bhsm,bmhd->bshd', p, v).reshape(b, s, d)
    return o @ w_o


def setup_inputs(seed: int = 0) -> dict:
    key = jax.random.key(seed)
    ks = iter(jax.random.split(key, 64))

    def normal(shape, scale):
        return scale * jax.random.normal(next(ks), shape, F32)

    def unif(shape, lo, hi):
        return jax.random.uniform(next(ks), shape, F32, lo, hi)

    def gain(shape):
        return 1.0 + normal(shape, 0.02)

    def dt_bias(shape):
        dt = jnp.exp(unif(shape, math.log(1e-3), math.log(1e-1)))
        return dt + jnp.log(-jnp.expm1(-dt))

    D = D_MODEL
    return {
        'x': normal((BATCH, SEQ, D), 1.0),
        'mem': normal((BATCH, N_MEM, D), 1.0),
        'sandwich_g': gain((DEPTH, N_SUBLAYERS, 2, D)),
        'ffn_w_in': normal((DEPTH, 2, D, 2 * D_FF), D ** -0.5),
        'ffn_w_out': normal((DEPTH, 2, D_FF, D), D_FF ** -0.5),
        'mem_norm_g': gain((DEPTH, D)),
        'xa_w_q': normal((DEPTH, D, D), D ** -0.5),
        'xa_w_kv': normal((DEPTH, D, 2 * D), D ** -0.5),
        'xa_w_o': normal((DEPTH, D, D), D ** -0.5),
        'dn_w_in': normal((N_DN, D, DN_IN), D ** -0.5),
        'dn_conv_w': normal((N_DN, CONV_W, DN_CONV_DIM), CONV_W ** -0.5),
        'dn_a_log': jnp.log(unif((N_DN, DN_V_HEADS), 1.0, 16.0)),
        'dn_dt_bias': dt_bias((N_DN, DN_V_HEADS)),
        'dn_norm_g': gain((N_DN, DN_DV)),
        'dn_w_out': normal((N_DN, DN_VAL, D), DN_VAL ** -0.5),
        'ssd_w_in': normal((N_SSD, D, SSD_IN), D ** -0.5),
        'ssd_conv_w': normal((N_SSD, CONV_W, SSD_CONV_DIM), CONV_W ** -0.5),
        'ssd_conv_b': normal((N_SSD, SSD_CONV_DIM), 0.02),
        'ssd_a_log': jnp.log(unif((N_SSD, SSD_HEADS), 1.0, 16.0)),
        'ssd_dt_bias': dt_bias((N_SSD, SSD_HEADS)),
        'ssd_d': gain((N_SSD, SSD_HEADS)),
        'ssd_norm_g': gain((N_SSD, SSD_INNER)),
        'ssd_w_out': normal((N_SSD, SSD_INNER, D), SSD_INNER ** -0.5),
        'rw_mu': unif((N_RW, N_LERP, D), 0.0, 1.0),
        'rw_w_rkv': normal((N_RW, 3, D, D), D ** -0.5),
        'rw_w0': unif((N_RW, D), -6.0, -1.0),
        'rw_w1': normal((N_RW, D, RW_DECAY_LORA), D ** -0.5),
        'rw_w2': normal((N_RW, RW_DECAY_LORA, D), 0.1 * RW_DECAY_LORA ** -0.5),
        'rw_a0': normal((N_RW, D), 0.1),
        'rw_a1': normal((N_RW, D, RW_A_LORA), D ** -0.5),
        'rw_a2': normal((N_RW, RW_A_LORA, D), RW_A_LORA ** -0.5),
        'rw_g1': normal((N_RW, D, RW_GATE_LORA), D ** -0.5),
        'rw_g2': normal((N_RW, RW_GATE_LORA, D), RW_GATE_LORA ** -0.5),
        'rw_k_k': 0.85 + normal((N_RW, D), 0.02),
        'rw_k_a': gain((N_RW, D)),
        'rw_r_k': normal((N_RW, RW_HEADS, RW_HEAD), 0.1),
        'rw_ln_g': gain((N_RW, D)),
        'rw_ln_b': normal((N_RW, D), 0.02),
        'rw_w_out': normal((N_RW, D, D), D ** -0.5),
    }


def reference(x, mem, sandwich_g, ffn_w_in, ffn_w_out, mem_norm_g, xa_w_q, xa_w_kv, xa_w_o,
              dn_w_in, dn_conv_w, dn_a_log, dn_dt_bias, dn_norm_g, dn_w_out,
              ssd_w_in, ssd_conv_w, ssd_conv_b, ssd_a_log, ssd_dt_bias, ssd_d, ssd_norm_g, ssd_w_out,
              rw_mu, rw_w_rkv, rw_w0, rw_w1, rw_w2, rw_a0, rw_a1, rw_a2, rw_g1, rw_g2,
              rw_k_k, rw_k_a, rw_r_k, rw_ln_g, rw_ln_b, rw_w_out):
    h = x
    for l in range(DEPTH):
        g = sandwich_g[l]
        kind = l % N_MIXERS
        j = l // N_MIXERS
        u = swiglu(rms_norm(h, g[0, 0]), ffn_w_in[l, 0], ffn_w_out[l, 0])
        h = h + MACARON_W * rms_norm(u, g[0, 1])
        u = rms_norm(h, g[1, 0])
        if kind == 0:
            u = gated_deltanet_mixer(u, dn_w_in[j], dn_conv_w[j], dn_a_log[j], dn_dt_bias[j],
                                     dn_norm_g[j], dn_w_out[j])
        elif kind == 1:
            u = mamba2_mixer(u, ssd_w_in[j], ssd_conv_w[j], ssd_conv_b[j], ssd_a_log[j],
                             ssd_dt_bias[j], ssd_d[j], ssd_norm_g[j], ssd_w_out[j])
        else:
            u = rwkv7_mixer(u, rw_mu[j], rw_w_rkv[j], rw_w0[j], rw_w1[j], rw_w2[j], rw_a0[j],
                            rw_a1[j], rw_a2[j], rw_g1[j], rw_g2[j], rw_k_k[j], rw_k_a[j],
                            rw_r_k[j], rw_ln_g[j], rw_ln_b[j], rw_w_out[j])
        h = h + rms_norm(u, g[1, 1])
        u = memory_cross_attention(rms_norm(h, g[2, 0]), rms_norm(mem, mem_norm_g[l]),
                                   xa_w_q[l], xa_w_kv[l], xa_w_o[l])
        h = h + rms_norm(u, g[2, 1])
        u = swiglu(rms_norm(h, g[3, 0]), ffn_w_in[l, 1], ffn_w_out[l, 1])
        h = h + MACARON_W * rms_norm(u, g[3, 1])
    return h
```

```python
import functools
import math

import jax
import jax.numpy as jnp
from jax import lax
from jax.experimental import pallas as pl
from jax.experimental.pallas import tpu as pltpu

F32 = jnp.float32
BF16 = jnp.bfloat16

D_MODEL = 1024
DEPTH = 4
N_MIXERS = 3
CHUNK = 64
CONV_W = 4
RMS_EPS = 1e-6
MACARON_W = 0.5
D_FF = 2816

DN_QK_HEADS = 8
DN_V_HEADS = 16
DN_DK = 128
DN_DV = 128
DN_Q = DN_QK_HEADS * DN_DK
DN_VAL = DN_V_HEADS * DN_DV
DN_CONV_DIM = 2 * DN_Q + DN_VAL

SSD_INNER = 2 * D_MODEL
SSD_HEADDIM = 64
SSD_HEADS = SSD_INNER // SSD_HEADDIM
SSD_GROUPS = 8
SSD_HPG = SSD_HEADS // SSD_GROUPS
SSD_STATE = 128
SSD_BC = SSD_GROUPS * SSD_STATE
SSD_CONV_DIM = SSD_INNER + 2 * SSD_BC

RW_HEAD = 64
RW_HEADS = D_MODEL // RW_HEAD
RW_GN_EPS = 64e-5

XA_HEADS = 4
XA_DH = D_MODEL // XA_HEADS

LANES = 128
HALO = 16
VMEM_LIMIT = 56 * 1024 * 1024
HI = lax.Precision.HIGHEST


def _cparams(*sem):
    return pltpu.CompilerParams(dimension_semantics=sem, vmem_limit_bytes=VMEM_LIMIT)


def _resident(shape):
    nd = len(shape)
    return pl.BlockSpec(shape, lambda *_: (0,) * nd, pipeline_mode=pl.Buffered(1))


def _rms(x, g, eps=RMS_EPS):
    return x * lax.rsqrt(jnp.mean(x * x, axis=-1, keepdims=True) + eps) * g


def _sigmoid(x):
    return 1.0 / (1.0 + jnp.exp(-x))


def _silu(x):
    return x * _sigmoid(x)


def _softplus(x):
    return jnp.maximum(x, 0.0) + jnp.log1p(jnp.exp(-jnp.abs(x)))


def _mm(a, b):
    return jnp.dot(a.astype(BF16), b.astype(BF16), preferred_element_type=F32)


def _mm_nt(a, b):
    return lax.dot_general(a.astype(BF16), b.astype(BF16), (((1,), (1,)), ((), ())),
                           preferred_element_type=F32)


def _mm_tn(a, b):
    return lax.dot_general(a.astype(BF16), b.astype(BF16), (((0,), (0,)), ((), ())),
                           preferred_element_type=F32)


def _mm_hi(a, b):
    return jnp.dot(a, b, preferred_element_type=F32, precision=HI)


def _tri_masks(n):
    row = lax.broadcasted_iota(jnp.int32, (n, n), 0)
    col = lax.broadcasted_iota(jnp.int32, (n, n), 1)
    return row, col


def _unit_lower_inverse(a, row, col):
    n = a.shape[0]
    eye = (row == col).astype(F32)
    x = eye - jnp.where((row >> 1) == (col >> 1), a, 0.0)
    b = 2
    while b < n:
        sh = b.bit_length() - 1
        m = ((row >> (sh + 1)) == (col >> (sh + 1))) & (((row >> sh) & 1) == 1) & (((col >> sh) & 1) == 0)
        lm = jnp.where(m, a, 0.0)
        x = x - _mm_hi(_mm_hi(x, lm), x)
        b *= 2
    return x


def _cumsum_rows(lower_ones, v):
    return _mm_hi(lower_ones, v)


def _norm_proj_kernel(x_ref, g_ref, w_ref, o_ref):
    xn = _rms(x_ref[...], g_ref[...]).astype(BF16)
    o_ref[...] = jnp.dot(xn, w_ref[...], preferred_element_type=F32).astype(o_ref.dtype)


def _norm_proj(x, g, w, *, tm, out_dtype=F32):
    t, d = x.shape
    n = w.shape[1]
    return pl.pallas_call(
        _norm_proj_kernel,
        grid=(t // tm,),
        in_specs=[pl.BlockSpec((tm, d), lambda i: (i, 0)), _resident((1, d)), _resident((d, n))],
        out_specs=pl.BlockSpec((tm, n), lambda i: (i, 0)),
        out_shape=jax.ShapeDtypeStruct((t, n), out_dtype),
        compiler_params=_cparams("parallel"),
        name="norm_proj",
    )(x, g, w)


def _norm_proj_conv_kernel(x_ref, xh_ref, g_ref, w_ref, cw_ref, cb_ref, o_ref, *, tm, tn, tiles_per_seq):
    i = pl.program_id(0)
    g = g_ref[...]
    first = (i % tiles_per_seq) == 0
    xh = jnp.where(first, 0.0, xh_ref[...])
    xn = jnp.concatenate([_rms(xh, g), _rms(x_ref[...], g)], axis=0).astype(BF16)
    n = w_ref.shape[1]
    for c in range(n // tn):
        sl = slice(c * tn, (c + 1) * tn)
        p = jnp.dot(xn, w_ref[:, sl], preferred_element_type=F32)
        cw = cw_ref[:, sl]
        y = cb_ref[:, sl] + cw[3:4] * p[HALO:HALO + tm]
        for j in range(CONV_W - 1):
            off = HALO - (CONV_W - 1) + j
            y = y + cw[j:j + 1] * p[off:off + tm]
        o_ref[:, sl] = _silu(y)


def _norm_proj_conv(x, g, w, conv_w, conv_b, *, seq, tm, tn=512):
    t, d = x.shape
    n = w.shape[1]
    hb = tm // HALO
    kern = functools.partial(_norm_proj_conv_kernel, tm=tm, tn=tn, tiles_per_seq=seq // tm)
    return pl.pallas_call(
        kern,
        grid=(t // tm,),
        in_specs=[pl.BlockSpec((tm, d), lambda i: (i, 0)),
                  pl.BlockSpec((HALO, d), lambda i: (jnp.maximum(i * hb - 1, 0), 0)),
                  _resident((1, d)), _resident((d, n)), _resident((CONV_W, n)), _resident((1, n))],
        out_specs=pl.BlockSpec((tm, n), lambda i: (i, 0)),
        out_shape=jax.ShapeDtypeStruct((t, n), F32),
        compiler_params=_cparams("parallel"),
        name="norm_proj_conv",
    )(x, x, g, w, conv_w, conv_b)


def _proj_post_kernel(a_ref, h_ref, w_ref, g_ref, o_ref, *, weight):
    u = jnp.dot(a_ref[...], w_ref[...], preferred_element_type=F32)
    o_ref[...] = h_ref[...] + weight * _rms(u, g_ref[...])


def _proj_post(a, h, w, g_post, *, tm, weight=1.0):
    t, k = a.shape
    d = w.shape[1]
    return pl.pallas_call(
        functools.partial(_proj_post_kernel, weight=weight),
        grid=(t // tm,),
        in_specs=[pl.BlockSpec((tm, k), lambda i: (i, 0)), pl.BlockSpec((tm, d), lambda i: (i, 0)),
                  _resident((k, d)), _resident((1, d))],
        out_specs=pl.BlockSpec((tm, d), lambda i: (i, 0)),
        out_shape=jax.ShapeDtypeStruct((t, d), F32),
        compiler_params=_cparams("parallel"),
        name="proj_post",
    )(a, h, w, g_post)


def _ffn_kernel(h_ref, gpre_ref, gpost_ref, wg_ref, wu_ref, wo_ref, o_ref, acc_ref, *, tf):
    h = h_ref[...]
    xn = _rms(h, gpre_ref[...]).astype(BF16)
    nf = wg_ref.shape[1] // tf
    for c in range(nf):
        sl = slice(c * tf, (c + 1) * tf)
        gate = jnp.dot(xn, wg_ref[:, sl], preferred_element_type=F32)
        up = jnp.dot(xn, wu_ref[:, sl], preferred_element_type=F32)
        act = (_silu(gate) * up).astype(BF16)
        part = jnp.dot(act, wo_ref[sl, :], preferred_element_type=F32)
        if c == 0:
            acc_ref[...] = part
        else:
            acc_ref[...] += part
    o_ref[...] = h + MACARON_W * _rms(acc_ref[...], gpost_ref[...])


def _ffn(h, g_pre, g_post, w_gate, w_up, w_out, *, tm, tf=256):
    t, d = h.shape
    f = w_gate.shape[1]
    return pl.pallas_call(
        functools.partial(_ffn_kernel, tf=tf),
        grid=(t // tm,),
        in_specs=[pl.BlockSpec((tm, d), lambda i: (i, 0)), _resident((1, d)), _resident((1, d)),
                  _resident((d, f)), _resident((d, f)), _resident((f, d))],
        out_specs=pl.BlockSpec((tm, d), lambda i: (i, 0)),
        out_shape=jax.ShapeDtypeStruct((t, d), F32),
        scratch_shapes=[pltpu.VMEM((tm, d), F32)],
        compiler_params=_cparams("parallel"),
        name="ffn",
    )(h, g_pre, g_post, w_gate, w_up, w_out)


def _xattn_kernel(h_ref, kv_ref, gpre_ref, gpost_ref, wq_ref, wo_ref, o_ref):
    h = h_ref[...]
    d = h.shape[1]
    xn = _rms(h, gpre_ref[...]).astype(BF16)
    q = jnp.dot(xn, wq_ref[...], preferred_element_type=F32) * XA_DH ** -0.5
    outs = []
    for hh in range(XA_HEADS):
        sl = slice(hh * XA_DH, (hh + 1) * XA_DH)
        k = kv_ref[:, sl]
        v = kv_ref[:, d + hh * XA_DH:d + (hh + 1) * XA_DH]
        s = _mm_nt(q[:, sl], k)
        s = s - jnp.max(s, axis=-1, keepdims=True)
        e = jnp.exp(s)
        p = e / jnp.sum(e, axis=-1, keepdims=True)
        outs.append(_mm(p, v))
    o = jnp.concatenate(outs, axis=-1).astype(BF16)
    u = jnp.dot(o, wo_ref[...], preferred_element_type=F32)
    o_ref[...] = h + _rms(u, gpost_ref[...])


def _xattn(h, kv, g_pre, g_post, w_q, w_o, *, seq, tm):
    t, d = h.shape
    n_mem = kv.shape[0] // (t // seq)
    tiles_per_seq = seq // tm
    return pl.pallas_call(
        _xattn_kernel,
        grid=(t // tm,),
        in_specs=[pl.BlockSpec((tm, d), lambda i: (i, 0)),
                  pl.BlockSpec((n_mem, 2 * d), lambda i: (i // tiles_per_seq, 0)),
                  _resident((1, d)), _resident((1, d)), _resident((d, d)), _resident((d, d))],
        out_specs=pl.BlockSpec((tm, d), lambda i: (i, 0)),
        out_shape=jax.ShapeDtypeStruct((t, d), F32),
        compiler_params=_cparams("parallel"),
        name="xattn",
    )(h, kv, g_pre, g_post, w_q, w_o)


def _dn_scan_kernel(qkv_ref, zb_ref, alog_ref, dtb_ref, ng_ref, o_ref, s_ref, *, tb):
    @pl.when(pl.program_id(1) == 0)
    def _():
        s_ref[...] = jnp.zeros_like(s_ref)

    row, col = _tri_masks(CHUNK)
    incl = col <= row
    strict = col < row
    lower_ones = incl.astype(F32)
    neg_a = -jnp.exp(alog_ref[...])
    dtb = dtb_ref[...]
    ng = ng_ref[...]

    def chunk(c, carry):
        r0 = pl.multiple_of(c * CHUNK, CHUNK)
        rows = pl.ds(r0, CHUNK)
        ba = zb_ref[rows, DN_VAL:DN_VAL + LANES]
        beta_all = _sigmoid(ba)
        g_all = neg_a * _softplus(ba + dtb)
        gc = _cumsum_rows(lower_ones, g_all)
        gct = gc.T
        eg_all = jnp.exp(gc)
        glast = gc[CHUNK - 1:CHUNK, :]
        eend_all = jnp.exp(glast - gc)
        elast_all = jnp.exp(glast)
        for hq in range(DN_QK_HEADS):
            q = qkv_ref[rows, hq * DN_DK:(hq + 1) * DN_DK]
            k = qkv_ref[rows, DN_Q + hq * DN_DK:DN_Q + (hq + 1) * DN_DK]
            q = q * lax.rsqrt(jnp.sum(q * q, axis=-1, keepdims=True) + 1e-6) * DN_DK ** -0.5
            k = k * lax.rsqrt(jnp.sum(k * k, axis=-1, keepdims=True) + 1e-6)
            kk = _mm_nt(k, k)
            qk = _mm_nt(q, k)
            for e in range(DN_V_HEADS // DN_QK_HEADS):
                h = hq * (DN_V_HEADS // DN_QK_HEADS) + e
                gl = DN_V_HEADS + h
                vsl = slice(h * DN_DV, (h + 1) * DN_DV)
                beta = beta_all[:, h:h + 1]
                decay = jnp.where(incl, jnp.exp(gc[:, gl:gl + 1] - gct[gl:gl + 1, :]), 0.0)
                a_mat = jnp.where(strict, kk * decay * beta, 0.0)
                t_inv = _unit_lower_inverse(a_mat, row, col)
                v = qkv_ref[rows, 2 * DN_Q + h * DN_DV:2 * DN_Q + (h + 1) * DN_DV]
                eg = eg_all[:, gl:gl + 1]
                rhs = jnp.concatenate([v * beta, k * (beta * eg)], axis=-1)
                sol = _mm(t_inv, rhs)
                u, wk = sol[:, :DN_DV], sol[:, DN_DV:]
                state = s_ref[h]
                v_new = u - _mm(wk, state)
                o = _mm(q * eg, state) + _mm(qk * decay, v_new)
                k_end = k * eend_all[:, gl:gl + 1]
                s_ref[h] = state * elast_all[:, gl:gl + 1] + _mm_tn(k_end, v_new)
                z = zb_ref[rows, vsl]
                o_ref[rows, vsl] = (_rms(o, ng) * _silu(z)).astype(o_ref.dtype)
        return carry

    lax.fori_loop(0, tb // CHUNK, chunk, 0)


def _dn_scan(qkv, zb, a_log_row, dt_bias_row, norm_g, *, batch, seq, tb):
    t = qkv.shape[0]
    nt = seq // tb
    return pl.pallas_call(
        functools.partial(_dn_scan_kernel, tb=tb),
        grid=(batch, nt),
        in_specs=[pl.BlockSpec((tb, DN_CONV_DIM), lambda b, i: (b * nt + i, 0)),
                  pl.BlockSpec((tb, DN_VAL + LANES), lambda b, i: (b * nt + i, 0)),
                  _resident((1, LANES)), _resident((1, LANES)), _resident((1, DN_DV))],
        out_specs=pl.BlockSpec((tb, DN_VAL), lambda b, i: (b * nt + i, 0)),
        out_shape=jax.ShapeDtypeStruct((t, DN_VAL), BF16),
        scratch_shapes=[pltpu.VMEM((DN_V_HEADS, DN_DK, DN_DV), F32)],
        compiler_params=_cparams("parallel", "arbitrary"),
        name="dn_scan",
    )(qkv, zb, a_log_row, dt_bias_row, norm_g)


def _ssd_scan_kernel(xbc_ref, zd_ref, alog_ref, dtb_ref, dskip_ref, ng_ref, o_ref, s_ref, *, tb):
    @pl.when(pl.program_id(1) == 0)
    def _():
        s_ref[...] = jnp.zeros_like(s_ref)

    row, col = _tri_masks(CHUNK)
    incl = col <= row
    lower_ones = incl.astype(F32)
    neg_a = -jnp.exp(alog_ref[...])
    dtb = dtb_ref[...]
    gsz = SSD_HPG * SSD_HEADDIM

    def chunk(c, carry):
        r0 = pl.multiple_of(c * CHUNK, CHUNK)
        rows = pl.ds(r0, CHUNK)
        dt_all = _softplus(zd_ref[rows, SSD_INNER:SSD_INNER + LANES] + dtb)
        ac = _cumsum_rows(lower_ones, dt_all * neg_a)
        act = ac.T
        ea_all = jnp.exp(ac)
        alast = ac[CHUNK - 1:CHUNK, :]
        toend_all = jnp.exp(alast - ac)
        elast_all = jnp.exp(alast)
        for g in range(SSD_GROUPS):
            gsl = slice(g * gsz, (g + 1) * gsz)
            bm = xbc_ref[rows, SSD_INNER + g * SSD_STATE:SSD_INNER + (g + 1) * SSD_STATE]
            cm = xbc_ref[rows, SSD_INNER + SSD_BC + g * SSD_STATE:SSD_INNER + SSD_BC + (g + 1) * SSD_STATE]
            xg = xbc_ref[rows, gsl]
            cb = _mm_nt(cm, bm)
            state = s_ref[g]
            y_state = _mm(cm, state)
            y_parts, xe_parts, ea_parts, el_parts = [], [], [], []
            for e in range(SSD_HPG):
                hh = g * SSD_HPG + e
                xdt = xg[:, e * SSD_HEADDIM:(e + 1) * SSD_HEADDIM] * dt_all[:, hh:hh + 1]
                seg = jnp.where(incl, jnp.exp(ac[:, hh:hh + 1] - act[hh:hh + 1, :]), 0.0)
                y_parts.append(_mm(cb * seg, xdt))
                xe_parts.append(xdt * toend_all[:, hh:hh + 1])
                ea_parts.append(jnp.broadcast_to(ea_all[:, hh:hh + 1], (CHUNK, SSD_HEADDIM)))
                el_parts.append(jnp.broadcast_to(elast_all[:, hh:hh + 1], (1, SSD_HEADDIM)))
            y = jnp.concatenate(y_parts, axis=-1) + y_state * jnp.concatenate(ea_parts, axis=-1)
            s_ref[g] = state * jnp.concatenate(el_parts, axis=-1) + _mm_tn(bm, jnp.concatenate(xe_parts, axis=-1))
            y = y + xg * dskip_ref[:, gsl]
            yz = y * _silu(zd_ref[rows, gsl])
            o_ref[rows, gsl] = _rms(yz, ng_ref[:, gsl]).astype(o_ref.dtype)
        return carry

    lax.fori_loop(0, tb // CHUNK, chunk, 0)


def _ssd_scan(xbc, zd, a_log_row, dt_bias_row, d_skip_row, norm_g, *, batch, seq, tb):
    t = xbc.shape[0]
    nt = seq // tb
    return pl.pallas_call(
        functools.partial(_ssd_scan_kernel, tb=tb),
        grid=(batch, nt),
        in_specs=[pl.BlockSpec((tb, SSD_CONV_DIM), lambda b, i: (b * nt + i, 0)),
                  pl.BlockSpec((tb, SSD_INNER + LANES), lambda b, i: (b * nt + i, 0)),
                  _resident((1, LANES)), _resident((1, LANES)),
                  _resident((1, SSD_INNER)), _resident((1, SSD_INNER))],
        out_specs=pl.BlockSpec((tb, SSD_INNER), lambda b, i: (b * nt + i, 0)),
        out_shape=jax.ShapeDtypeStruct((t, SSD_INNER), BF16),
        scratch_shapes=[pltpu.VMEM((SSD_GROUPS, SSD_STATE, SSD_HPG * SSD_HEADDIM), F32)],
        compiler_params=_cparams("parallel", "arbitrary"),
        name="ssd_scan",
    )(xbc, zd, a_log_row, dt_bias_row, d_skip_row, norm_g)


def _rw_proj_kernel(x_ref, xh_ref, gpre_ref, mu_ref, wrkv_ref, w0_ref, w1_ref, w2_ref, a0_ref, a1_ref, a2_ref,
                    g1_ref, g2_ref, kk_ref, ka_ref,
                    r_out, lw_out, k_out, v_out, kk_out, a_out, gate_out, *, tm, tiles_per_seq):
    i = pl.program_id(0)
    g = gpre_ref[...]
    first = (i % tiles_per_seq) == 0
    x = _rms(x_ref[...], g)
    xh = _rms(jnp.where(first, 0.0, xh_ref[...]), g)
    prev = jnp.concatenate([xh, x], axis=0)[HALO - 1:HALO - 1 + tm]
    xx = prev - x
    mu = mu_ref[...]
    xr, xw, xk, xv, xa, xg = [(x + xx * mu[j:j + 1]) for j in range(6)]
    r = _mm(xr, wrkv_ref[0])
    k = _mm(xk, wrkv_ref[1])
    v = _mm(xv, wrkv_ref[2])
    w = -_softplus(-(w0_ref[...] + _mm(jnp.tanh(_mm(xw, w1_ref[...])), w2_ref[...]))) - 0.5
    a = _sigmoid(a0_ref[...] + _mm(_mm(xa, a1_ref[...]), a2_ref[...]))
    gate = _mm(_sigmoid(_mm(xg, g1_ref[...])), g2_ref[...])
    r_out[...] = r
    lw_out[...] = -jnp.exp(w)
    k_out[...] = k * (1.0 + (a - 1.0) * ka_ref[...])
    v_out[...] = v
    kk_out[...] = k * kk_ref[...]
    a_out[...] = a
    gate_out[...] = gate


def _rw_proj(h, g_pre, mu, w_rkv, w0, w1, w2, a0, a1, a2, g1, g2, k_k, k_a, *, seq, tm):
    t, d = h.shape
    hb = tm // HALO
    row = lambda: _resident((1, d))
    tok = pl.BlockSpec((tm, d), lambda i: (i, 0))
    out = jax.ShapeDtypeStruct((t, d), F32)
    return pl.pallas_call(
        functools.partial(_rw_proj_kernel, tm=tm, tiles_per_seq=seq // tm),
        grid=(t // tm,),
        in_specs=[tok, pl.BlockSpec((HALO, d), lambda i: (jnp.maximum(i * hb - 1, 0), 0)),
                  row(), _resident(mu.shape), _resident(w_rkv.shape), row(), _resident(w1.shape),
                  _resident(w2.shape), row(), _resident(a1.shape), _resident(a2.shape),
                  _resident(g1.shape), _resident(g2.shape), row(), row()],
        out_specs=[tok] * 7,
        out_shape=[out] * 7,
        compiler_params=_cparams("parallel"),
        name="rw_proj",
    )(h, h, g_pre, mu, w_rkv, w0, w1, w2, a0, a1, a2, g1, g2, k_k, k_a)


def _rw_scan_kernel(r_ref, lw_ref, k_ref, v_ref, kk_ref, a_ref, gate_ref, rk_ref, lng_ref, lnb_ref,
                    o_ref, s_ref, *, tb):
    @pl.when(pl.program_id(1) == 0)
    def _():
        s_ref[...] = jnp.zeros_like(s_ref)

    row, col = _tri_masks(CHUNK)
    incl = col <= row
    strict = col < row
    lower_ones = incl.astype(F32)
    n = RW_HEAD

    def chunk(c, carry):
        r0 = pl.multiple_of(c * CHUNK, CHUNK)
        rows = pl.ds(r0, CHUNK)
        for hp in range(RW_HEADS // 2):
            bsl = slice(hp * LANES, (hp + 1) * LANES)
            lw = lw_ref[rows, bsl]
            wc = _cumsum_rows(lower_ones, lw)
            e_in = jnp.exp(wc)
            e_ex = jnp.exp(wc - lw)
            e_inv = jnp.exp(-wc)
            e_last = jnp.exp(wc[CHUNK - 1:CHUNK, :])
            r = r_ref[rows, bsl]
            k = k_ref[rows, bsl]
            v = v_ref[rows, bsl]
            kk2 = kk_ref[rows, bsl]
            a2 = a_ref[rows, bsl]
            rt2 = r * e_in
            kt2 = k * e_inv
            rkv = r * k * rk_ref[:, bsl]
            ys = []
            for e in range(2):
                h = hp * 2 + e
                sl = slice(e * n, (e + 1) * n)
                rt, kt, vh = rt2[:, sl], kt2[:, sl], v[:, sl]
                kk = kk2[:, sl]
                kk = kk * lax.rsqrt(jnp.sum(kk * kk, axis=-1, keepdims=True) + 1e-6)
                at = -kk * e_ex[:, sl]
                bt = kk * a2[:, sl] * e_inv[:, sl]
                state = s_ref[h]
                a_ab = jnp.where(strict, _mm_nt(at, bt), 0.0)
                a_ak = jnp.where(strict, _mm_nt(at, kt), 0.0)
                t_inv = _unit_lower_inverse(-a_ab, row, col)
                u = _mm(t_inv, _mm_nt(at, state) + _mm(a_ak, vh))
                y = (_mm_nt(rt, state) + _mm(jnp.where(incl, _mm_nt(rt, bt), 0.0), u)
                     + _mm(jnp.where(incl, _mm_nt(rt, kt), 0.0), vh))
                s_ref[h] = (state + _mm_tn(u, bt) + _mm_tn(vh, kt)) * e_last[:, sl]
                mean = jnp.mean(y, axis=-1, keepdims=True)
                yc = y - mean
                var = jnp.mean(yc * yc, axis=-1, keepdims=True)
                yn = yc * lax.rsqrt(var + RW_GN_EPS)
                bonus = jnp.sum(rkv[:, sl], axis=-1, keepdims=True) * vh
                ys.append((yn, bonus))
            yn = jnp.concatenate([ys[0][0], ys[1][0]], axis=-1)
            bonus = jnp.concatenate([ys[0][1], ys[1][1]], axis=-1)
            out = (yn * lng_ref[:, bsl] + lnb_ref[:, bsl] + bonus) * gate_ref[rows, bsl]
            o_ref[rows, bsl] = out.astype(o_ref.dtype)
        return carry

    lax.fori_loop(0, tb // CHUNK, chunk, 0)


def _rw_scan(r, lw, k, v, kk, a, gate, r_k, ln_g, ln_b, *, batch, seq, tb):
    t, d = r.shape
    nt = seq // tb
    tok = pl.BlockSpec((tb, d), lambda bi, i: (bi * nt + i, 0))
    return pl.pallas_call(
        functools.partial(_rw_scan_kernel, tb=tb),
        grid=(batch, nt),
        in_specs=[tok] * 7 + [_resident((1, d))] * 3,
        out_specs=tok,
        out_shape=jax.ShapeDtypeStruct((t, d), BF16),
        scratch_shapes=[pltpu.VMEM((RW_HEADS, RW_HEAD, RW_HEAD), F32)],
        compiler_params=_cparams("parallel", "arbitrary"),
        name="rw_scan",
    )(r, lw, k, v, kk, a, gate, r_k, ln_g, ln_b)


def _row(v):
    return v.reshape(1, -1).astype(F32)


def _lane_row(v, offset):
    return jnp.zeros((1, LANES), F32).at[0, offset:offset + v.shape[0]].set(v.astype(F32))


def _pad_cols(w, n):
    return jnp.pad(w, ((0, 0), (0, n - w.shape[1])))


def _tile(seq, want):
    return min(seq, want)


def _deltanet_layer(h, g_pre, g_post, w_in, conv_w, a_log, dt_bias, norm_g, w_out, *, batch, seq):
    w_conv = w_in[:, :DN_CONV_DIM].astype(BF16)
    w_rest = _pad_cols(w_in[:, DN_CONV_DIM:], DN_VAL + LANES).astype(BF16)
    qkv = _norm_proj_conv(h, g_pre, w_conv, conv_w, jnp.zeros((1, DN_CONV_DIM), F32),
                          seq=seq, tm=_tile(seq, 512))
    zb = _norm_proj(h, g_pre, w_rest, tm=_tile(seq, 512))
    o = _dn_scan(qkv, zb, _lane_row(a_log, DN_V_HEADS), _lane_row(dt_bias, DN_V_HEADS), _row(norm_g),
                 batch=batch, seq=seq, tb=_tile(seq, 256))
    return _proj_post(o, h, w_out.astype(BF16), g_post, tm=_tile(seq, 512))


def _ssd_layer(h, g_pre, g_post, w_in, conv_w, conv_b, a_log, dt_bias, d_skip, norm_g, w_out, *, batch, seq):
    w_conv = w_in[:, SSD_INNER:SSD_INNER + SSD_CONV_DIM].astype(BF16)
    w_rest = _pad_cols(jnp.concatenate([w_in[:, :SSD_INNER], w_in[:, SSD_INNER + SSD_CONV_DIM:]], axis=1),
                       SSD_INNER + LANES).astype(BF16)
    xbc = _norm_proj_conv(h, g_pre, w_conv, conv_w, _row(conv_b), seq=seq, tm=_tile(seq, 512))
    zd = _norm_proj(h, g_pre, w_rest, tm=_tile(seq, 512))
    y = _ssd_scan(xbc, zd, _lane_row(a_log, 0), _lane_row(dt_bias, 0),
                  _row(jnp.repeat(d_skip, SSD_HEADDIM)), _row(norm_g), batch=batch, seq=seq, tb=_tile(seq, 256))
    return _proj_post(y, h, w_out.astype(BF16), g_post, tm=_tile(seq, 512))


def _rwkv_layer(h, g_pre, g_post, mu, w_rkv, w0, w1, w2, a0, a1, a2, g1, g2, k_k, k_a, r_k, ln_g, ln_b, w_out,
                *, batch, seq):
    bf = lambda w: w.astype(BF16)
    outs = _rw_proj(h, g_pre, mu.astype(F32), bf(w_rkv), _row(w0), bf(w1), bf(w2), _row(a0), bf(a1), bf(a2),
                    bf(g1), bf(g2), _row(k_k), _row(k_a), seq=seq, tm=_tile(seq, 256))
    y = _rw_scan(*outs, _row(r_k), _row(ln_g), _row(ln_b), batch=batch, seq=seq, tb=_tile(seq, 256))
    return _proj_post(y, h, bf(w_out), g_post, tm=_tile(seq, 512))


def kernel(x, mem, sandwich_g, ffn_w_in, ffn_w_out, mem_norm_g, xa_w_q, xa_w_kv, xa_w_o, dn_w_in, dn_conv_w, dn_a_log, dn_dt_bias, dn_norm_g, dn_w_out, ssd_w_in, ssd_conv_w, ssd_conv_b, ssd_a_log, ssd_dt_bias, ssd_d, ssd_norm_g, ssd_w_out, rw_mu, rw_w_rkv, rw_w0, rw_w1, rw_w2, rw_a0, rw_a1, rw_a2, rw_g1, rw_g2, rw_k_k, rw_k_a, rw_r_k, rw_ln_g, rw_ln_b, rw_w_out):
    batch, seq, d = x.shape
    n_mem = mem.shape[1]
    h = x.reshape(batch * seq, d)
    mem2 = mem.reshape(batch * n_mem, d)
    tm_ffn = _tile(seq, 512)

    def ffn(h, l, which, g_pre, g_post):
        w_in = ffn_w_in[l, which].astype(BF16)
        return _ffn(h, _row(g_pre), _row(g_post), w_in[:, :D_FF], w_in[:, D_FF:],
                    ffn_w_out[l, which].astype(BF16), tm=tm_ffn)

    for l in range(DEPTH):
        g = sandwich_g[l]
        kind = l % N_MIXERS
        j = l // N_MIXERS
        h = ffn(h, l, 0, g[0, 0], g[0, 1])
        gp, gq = _row(g[1, 0]), _row(g[1, 1])
        if kind == 0:
            h = _deltanet_layer(h, gp, gq, dn_w_in[j], dn_conv_w[j], dn_a_log[j], dn_dt_bias[j], dn_norm_g[j],
                                dn_w_out[j], batch=batch, seq=seq)
        elif kind == 1:
            h = _ssd_layer(h, gp, gq, ssd_w_in[j], ssd_conv_w[j], ssd_conv_b[j], ssd_a_log[j], ssd_dt_bias[j],
                           ssd_d[j], ssd_norm_g[j], ssd_w_out[j], batch=batch, seq=seq)
        else:
            h = _rwkv_layer(h, gp, gq, rw_mu[j], rw_w_rkv[j], rw_w0[j], rw_w1[j], rw_w2[j], rw_a0[j], rw_a1[j],
                            rw_a2[j], rw_g1[j], rw_g2[j], rw_k_k[j], rw_k_a[j], rw_r_k[j].reshape(-1),
                            rw_ln_g[j], rw_ln_b[j], rw_w_out[j], batch=batch, seq=seq)
        kv = _norm_proj(mem2, _row(mem_norm_g[l]), xa_w_kv[l].astype(BF16), tm=n_mem, out_dtype=BF16)
        h = _xattn(h, kv, _row(g[2, 0]), _row(g[2, 1]), xa_w_q[l].astype(BF16), xa_w_o[l].astype(BF16),
                   seq=seq, tm=_tile(seq, 512))
        h = ffn(h, l, 1, g[3, 0], g[3, 1])
    return h.reshape(batch, seq, d)
```

```python
import functools
import math

import jax
import jax.numpy as jnp
from jax import lax
from jax.experimental import pallas as pl
from jax.experimental.pallas import tpu as pltpu

F32 = jnp.float32
BF16 = jnp.bfloat16

D_MODEL = 1024
DEPTH = 4
N_MIXERS = 3
CHUNK = 64
CONV_W = 4
RMS_EPS = 1e-6
MACARON_W = 0.5
D_FF = 2816

DN_QK_HEADS = 8
DN_V_HEADS = 16
DN_DK = 128
DN_DV = 128
DN_Q = DN_QK_HEADS * DN_DK
DN_VAL = DN_V_HEADS * DN_DV
DN_CONV_DIM = 2 * DN_Q + DN_VAL

SSD_INNER = 2 * D_MODEL
SSD_HEADDIM = 64
SSD_HEADS = SSD_INNER // SSD_HEADDIM
SSD_GROUPS = 8
SSD_HPG = SSD_HEADS // SSD_GROUPS
SSD_STATE = 128
SSD_BC = SSD_GROUPS * SSD_STATE
SSD_CONV_DIM = SSD_INNER + 2 * SSD_BC

RW_HEAD = 64
RW_HEADS = D_MODEL // RW_HEAD
RW_GN_EPS = 64e-5

XA_HEADS = 4
XA_DH = D_MODEL // XA_HEADS

LANES = 128
HALO = 16
VMEM_LIMIT = 56 * 1024 * 1024
HI = lax.Precision.HIGHEST


def _cparams(*sem):
    return pltpu.CompilerParams(dimension_semantics=sem, vmem_limit_bytes=VMEM_LIMIT)


def _resident(shape):
    nd = len(shape)
    return pl.BlockSpec(shape, lambda *_: (0,) * nd, pipeline_mode=pl.Buffered(1))


def _rms(x, g, eps=RMS_EPS):
    return x * lax.rsqrt(jnp.mean(x * x, axis=-1, keepdims=True) + eps) * g


def _sigmoid(x):
    return 1.0 / (1.0 + jnp.exp(-x))


def _silu(x):
    return x * _sigmoid(x)


def _softplus(x):
    return jnp.maximum(x, 0.0) + jnp.log1p(jnp.exp(-jnp.abs(x)))


def _mm(a, b):
    return jnp.dot(a.astype(BF16), b.astype(BF16), preferred_element_type=F32)


def _mm_nt(a, b):
    return lax.dot_general(a.astype(BF16), b.astype(BF16), (((1,), (1,)), ((), ())),
                           preferred_element_type=F32)


def _mm_tn(a, b):
    return lax.dot_general(a.astype(BF16), b.astype(BF16), (((0,), (0,)), ((), ())),
                           preferred_element_type=F32)


def _mm_hi(a, b):
    return jnp.dot(a, b, preferred_element_type=F32, precision=HI)


def _tri_masks(n):
    row = lax.broadcasted_iota(jnp.int32, (n, n), 0)
    col = lax.broadcasted_iota(jnp.int32, (n, n), 1)
    return row, col


_NN = ((2,), (1,))
_NT = ((2,), (2,))
_TN = ((1,), (1,))


def _bmm(a, b, dims=_NN):
    return lax.dot_general(a.astype(BF16), b.astype(BF16), (dims, ((0,), (0,))), preferred_element_type=F32)


def _unit_lower_inverse(a, row, col):
    n = a.shape[-1]
    eye = (row == col).astype(F32)
    x = eye - jnp.where((row >> 1) == (col >> 1), a, 0.0)
    b = 2
    while b < n:
        sh = b.bit_length() - 1
        m = ((row >> (sh + 1)) == (col >> (sh + 1))) & (((row >> sh) & 1) == 1) & (((col >> sh) & 1) == 0)
        lm = jnp.where(m, a, 0.0)
        x = x - _bmm(_bmm(x, lm), x)
        b *= 2
    return x


def _cumsum_rows(lower_ones, v):
    return _mm_hi(lower_ones, v)


def _norm_proj_kernel(x_ref, g_ref, w_ref, o_ref):
    xn = _rms(x_ref[...], g_ref[...]).astype(BF16)
    o_ref[...] = jnp.dot(xn, w_ref[...], preferred_element_type=F32).astype(o_ref.dtype)


def _norm_proj(x, g, w, *, tm, out_dtype=F32):
    t, d = x.shape
    n = w.shape[1]
    return pl.pallas_call(
        _norm_proj_kernel,
        grid=(t // tm,),
        in_specs=[pl.BlockSpec((tm, d), lambda i: (i, 0)), _resident((1, d)), _resident((d, n))],
        out_specs=pl.BlockSpec((tm, n), lambda i: (i, 0)),
        out_shape=jax.ShapeDtypeStruct((t, n), out_dtype),
        compiler_params=_cparams("parallel"),
        name="norm_proj",
    )(x, g, w)


def _norm_proj_conv_kernel(x_ref, xh_ref, g_ref, w_ref, cw_ref, cb_ref, o_ref, *, tm, tn, tiles_per_seq):
    i = pl.program_id(0)
    g = g_ref[...]
    first = (i % tiles_per_seq) == 0
    xh = jnp.where(first, 0.0, xh_ref[...])
    xn = jnp.concatenate([_rms(xh, g), _rms(x_ref[...], g)], axis=0).astype(BF16)
    n = w_ref.shape[1]
    for c in range(n // tn):
        sl = slice(c * tn, (c + 1) * tn)
        p = jnp.dot(xn, w_ref[:, sl], preferred_element_type=F32)
        cw = cw_ref[:, sl]
        y = cb_ref[:, sl] + cw[3:4] * p[HALO:HALO + tm]
        for j in range(CONV_W - 1):
            off = HALO - (CONV_W - 1) + j
            y = y + cw[j:j + 1] * p[off:off + tm]
        o_ref[:, sl] = _silu(y)


def _norm_proj_conv(x, g, w, conv_w, conv_b, *, seq, tm, tn=512):
    t, d = x.shape
    n = w.shape[1]
    hb = tm // HALO
    kern = functools.partial(_norm_proj_conv_kernel, tm=tm, tn=tn, tiles_per_seq=seq // tm)
    return pl.pallas_call(
        kern,
        grid=(t // tm,),
        in_specs=[pl.BlockSpec((tm, d), lambda i: (i, 0)),
                  pl.BlockSpec((HALO, d), lambda i: (jnp.maximum(i * hb - 1, 0), 0)),
                  _resident((1, d)), _resident((d, n)), _resident((CONV_W, n)), _resident((1, n))],
        out_specs=pl.BlockSpec((tm, n), lambda i: (i, 0)),
        out_shape=jax.ShapeDtypeStruct((t, n), F32),
        compiler_params=_cparams("parallel"),
        name="norm_proj_conv",
    )(x, x, g, w, conv_w, conv_b)


def _proj_post_kernel(a_ref, h_ref, w_ref, g_ref, o_ref, *, weight):
    u = jnp.dot(a_ref[...], w_ref[...], preferred_element_type=F32)
    o_ref[...] = h_ref[...] + weight * _rms(u, g_ref[...])


def _proj_post(a, h, w, g_post, *, tm, weight=1.0):
    t, k = a.shape
    d = w.shape[1]
    return pl.pallas_call(
        functools.partial(_proj_post_kernel, weight=weight),
        grid=(t // tm,),
        in_specs=[pl.BlockSpec((tm, k), lambda i: (i, 0)), pl.BlockSpec((tm, d), lambda i: (i, 0)),
                  _resident((k, d)), _resident((1, d))],
        out_specs=pl.BlockSpec((tm, d), lambda i: (i, 0)),
        out_shape=jax.ShapeDtypeStruct((t, d), F32),
        compiler_params=_cparams("parallel"),
        name="proj_post",
    )(a, h, w, g_post)


def _ffn_kernel(h_ref, gpre_ref, gpost_ref, wg_ref, wu_ref, wo_ref, o_ref, acc_ref, *, tf):
    h = h_ref[...]
    xn = _rms(h, gpre_ref[...]).astype(BF16)
    nf = wg_ref.shape[1] // tf
    for c in range(nf):
        sl = slice(c * tf, (c + 1) * tf)
        gate = jnp.dot(xn, wg_ref[:, sl], preferred_element_type=F32)
        up = jnp.dot(xn, wu_ref[:, sl], preferred_element_type=F32)
        act = (_silu(gate) * up).astype(BF16)
        part = jnp.dot(act, wo_ref[sl, :], preferred_element_type=F32)
        if c == 0:
            acc_ref[...] = part
        else:
            acc_ref[...] += part
    o_ref[...] = h + MACARON_W * _rms(acc_ref[...], gpost_ref[...])


def _ffn(h, g_pre, g_post, w_gate, w_up, w_out, *, tm, tf=256):
    t, d = h.shape
    f = w_gate.shape[1]
    return pl.pallas_call(
        functools.partial(_ffn_kernel, tf=tf),
        grid=(t // tm,),
        in_specs=[pl.BlockSpec((tm, d), lambda i: (i, 0)), _resident((1, d)), _resident((1, d)),
                  _resident((d, f)), _resident((d, f)), _resident((f, d))],
        out_specs=pl.BlockSpec((tm, d), lambda i: (i, 0)),
        out_shape=jax.ShapeDtypeStruct((t, d), F32),
        scratch_shapes=[pltpu.VMEM((tm, d), F32)],
        compiler_params=_cparams("parallel"),
        name="ffn",
    )(h, g_pre, g_post, w_gate, w_up, w_out)


def _xattn_kernel(h_ref, kv_ref, gpre_ref, gpost_ref, wq_ref, wo_ref, o_ref):
    h = h_ref[...]
    d = h.shape[1]
    xn = _rms(h, gpre_ref[...]).astype(BF16)
    q = jnp.dot(xn, wq_ref[...], preferred_element_type=F32) * XA_DH ** -0.5
    outs = []
    for hh in range(XA_HEADS):
        sl = slice(hh * XA_DH, (hh + 1) * XA_DH)
        k = kv_ref[:, sl]
        v = kv_ref[:, d + hh * XA_DH:d + (hh + 1) * XA_DH]
        s = _mm_nt(q[:, sl], k)
        s = s - jnp.max(s, axis=-1, keepdims=True)
        e = jnp.exp(s)
        p = e / jnp.sum(e, axis=-1, keepdims=True)
        outs.append(_mm(p, v))
    o = jnp.concatenate(outs, axis=-1).astype(BF16)
    u = jnp.dot(o, wo_ref[...], preferred_element_type=F32)
    o_ref[...] = h + _rms(u, gpost_ref[...])


def _xattn(h, kv, g_pre, g_post, w_q, w_o, *, seq, tm):
    t, d = h.shape
    n_mem = kv.shape[0] // (t // seq)
    tiles_per_seq = seq // tm
    return pl.pallas_call(
        _xattn_kernel,
        grid=(t // tm,),
        in_specs=[pl.BlockSpec((tm, d), lambda i: (i, 0)),
                  pl.BlockSpec((n_mem, 2 * d), lambda i: (i // tiles_per_seq, 0)),
                  _resident((1, d)), _resident((1, d)), _resident((d, d)), _resident((d, d))],
        out_specs=pl.BlockSpec((tm, d), lambda i: (i, 0)),
        out_shape=jax.ShapeDtypeStruct((t, d), F32),
        compiler_params=_cparams("parallel"),
        name="xattn",
    )(h, kv, g_pre, g_post, w_q, w_o)


def _dn_scan_kernel(qkv_ref, zb_ref, alog_ref, dtb_ref, ng_ref, o_ref, s_ref, *, tb):
    @pl.when(pl.program_id(1) == 0)
    def _():
        s_ref[...] = jnp.zeros_like(s_ref)

    row, col = _tri_masks(CHUNK)
    incl = col <= row
    strict = col < row
    lower_ones = incl.astype(F32)
    neg_a = -jnp.exp(alog_ref[...])
    dtb = dtb_ref[...]
    ng = ng_ref[...]

    def chunk(c, carry):
        r0 = pl.multiple_of(c * CHUNK, CHUNK)
        rows = pl.ds(r0, CHUNK)
        ba = zb_ref[rows, DN_VAL:DN_VAL + LANES]
        beta_all = _sigmoid(ba)
        g_all = neg_a * _softplus(ba + dtb)
        gc = _cumsum_rows(lower_ones, g_all)
        gct = gc.T
        eg_all = jnp.exp(gc)
        glast = gc[CHUNK - 1:CHUNK, :]
        eend_all = jnp.exp(glast - gc)
        elast_all = jnp.exp(glast)
        qs, ks = [], []
        for hq in range(DN_QK_HEADS):
            q = qkv_ref[rows, hq * DN_DK:(hq + 1) * DN_DK]
            k = qkv_ref[rows, DN_Q + hq * DN_DK:DN_Q + (hq + 1) * DN_DK]
            qs.append(q * lax.rsqrt(jnp.sum(q * q, axis=-1, keepdims=True) + 1e-6) * DN_DK ** -0.5)
            ks.append(k * lax.rsqrt(jnp.sum(k * k, axis=-1, keepdims=True) + 1e-6))
        k8 = jnp.stack(ks)
        kk8 = _bmm(k8, k8, _NT)
        qk8 = _bmm(jnp.stack(qs), k8, _NT)
        a_l, att_l, rhs_l, qe_l, kend_l, elast_l = [], [], [], [], [], []
        for h in range(DN_V_HEADS):
            hq = h // (DN_V_HEADS // DN_QK_HEADS)
            gl = DN_V_HEADS + h
            beta = beta_all[:, h:h + 1]
            eg = eg_all[:, gl:gl + 1]
            decay = jnp.where(incl, jnp.exp(gc[:, gl:gl + 1] - gct[gl:gl + 1, :]), 0.0)
            a_l.append(jnp.where(strict, kk8[hq] * decay * beta, 0.0))
            att_l.append(qk8[hq] * decay)
            v = qkv_ref[rows, 2 * DN_Q + h * DN_DV:2 * DN_Q + (h + 1) * DN_DV]
            rhs_l.append(jnp.concatenate([v * beta, ks[hq] * (beta * eg)], axis=-1))
            qe_l.append(qs[hq] * eg)
            kend_l.append(ks[hq] * eend_all[:, gl:gl + 1])
            elast_l.append(elast_all[:, gl:gl + 1])
        t_inv = _unit_lower_inverse(jnp.stack(a_l), row, col)
        sol = _bmm(t_inv, jnp.stack(rhs_l))
        u, wk = sol[:, :, :DN_DV], sol[:, :, DN_DV:]
        state = s_ref[...]
        v_new = u - _bmm(wk, state)
        o = _bmm(jnp.stack(qe_l), state) + _bmm(jnp.stack(att_l), v_new)
        s_ref[...] = state * jnp.stack(elast_l) + _bmm(jnp.stack(kend_l), v_new, _TN)
        for h in range(DN_V_HEADS):
            vsl = slice(h * DN_DV, (h + 1) * DN_DV)
            z = zb_ref[rows, vsl]
            o_ref[rows, vsl] = (_rms(o[h], ng) * _silu(z)).astype(o_ref.dtype)
        return carry

    lax.fori_loop(0, tb // CHUNK, chunk, 0, unroll=2)


def _dn_scan(qkv, zb, a_log_row, dt_bias_row, norm_g, *, batch, seq, tb):
    t = qkv.shape[0]
    nt = seq // tb
    return pl.pallas_call(
        functools.partial(_dn_scan_kernel, tb=tb),
        grid=(batch, nt),
        in_specs=[pl.BlockSpec((tb, DN_CONV_DIM), lambda b, i: (b * nt + i, 0)),
                  pl.BlockSpec((tb, DN_VAL + LANES), lambda b, i: (b * nt + i, 0)),
                  _resident((1, LANES)), _resident((1, LANES)), _resident((1, DN_DV))],
        out_specs=pl.BlockSpec((tb, DN_VAL), lambda b, i: (b * nt + i, 0)),
        out_shape=jax.ShapeDtypeStruct((t, DN_VAL), BF16),
        scratch_shapes=[pltpu.VMEM((DN_V_HEADS, DN_DK, DN_DV), F32)],
        compiler_params=_cparams("parallel", "arbitrary"),
        name="dn_scan",
    )(qkv, zb, a_log_row, dt_bias_row, norm_g)


def _ssd_scan_kernel(xbc_ref, zd_ref, alog_ref, dtb_ref, dskip_ref, ng_ref, o_ref, s_ref, *, tb):
    @pl.when(pl.program_id(1) == 0)
    def _():
        s_ref[...] = jnp.zeros_like(s_ref)

    row, col = _tri_masks(CHUNK)
    incl = col <= row
    lower_ones = incl.astype(F32)
    neg_a = -jnp.exp(alog_ref[...])
    dtb = dtb_ref[...]
    gsz = SSD_HPG * SSD_HEADDIM

    def chunk(c, carry):
        r0 = pl.multiple_of(c * CHUNK, CHUNK)
        rows = pl.ds(r0, CHUNK)
        dt_all = _softplus(zd_ref[rows, SSD_INNER:SSD_INNER + LANES] + dtb)
        ac = _cumsum_rows(lower_ones, dt_all * neg_a)
        act = ac.T
        ea_all = jnp.exp(ac)
        alast = ac[CHUNK - 1:CHUNK, :]
        toend_all = jnp.exp(alast - ac)
        elast_all = jnp.exp(alast)
        bm = jnp.stack([xbc_ref[rows, SSD_INNER + g * SSD_STATE:SSD_INNER + (g + 1) * SSD_STATE]
                        for g in range(SSD_GROUPS)])
        cm = jnp.stack([xbc_ref[rows, SSD_INNER + SSD_BC + g * SSD_STATE:SSD_INNER + SSD_BC + (g + 1) * SSD_STATE]
                        for g in range(SSD_GROUPS)])
        cb = _bmm(cm, bm, _NT)
        state = s_ref[...]
        y_state = _bmm(cm, state)
        xs = xbc_ref[rows, 0:SSD_INNER]
        w_l, xdt_l, xe_l, ea_l, el_l = [], [], [], [], []
        for hh in range(SSD_HEADS):
            xdt = xs[:, hh * SSD_HEADDIM:(hh + 1) * SSD_HEADDIM] * dt_all[:, hh:hh + 1]
            seg = jnp.where(incl, jnp.exp(ac[:, hh:hh + 1] - act[hh:hh + 1, :]), 0.0)
            w_l.append(cb[hh // SSD_HPG] * seg)
            xdt_l.append(xdt)
            xe_l.append(xdt * toend_all[:, hh:hh + 1])
            ea_l.append(jnp.broadcast_to(ea_all[:, hh:hh + 1], (CHUNK, SSD_HEADDIM)))
            el_l.append(jnp.broadcast_to(elast_all[:, hh:hh + 1], (1, SSD_HEADDIM)))
        y_intra = _bmm(jnp.stack(w_l), jnp.stack(xdt_l))
        grp = lambda parts, g: jnp.concatenate(parts[g * SSD_HPG:(g + 1) * SSD_HPG], axis=-1)
        xe = jnp.stack([grp(xe_l, g) for g in range(SSD_GROUPS)])
        el = jnp.stack([grp(el_l, g) for g in range(SSD_GROUPS)])
        s_ref[...] = state * el + _bmm(bm, xe, _TN)
        for g in range(SSD_GROUPS):
            gsl = slice(g * gsz, (g + 1) * gsz)
            y = grp([y_intra[hh] for hh in range(SSD_HEADS)], g) + y_state[g] * grp(ea_l, g)
            y = y + xs[:, gsl] * dskip_ref[:, gsl]
            yz = y * _silu(zd_ref[rows, gsl])
            o_ref[rows, gsl] = _rms(yz, ng_ref[:, gsl]).astype(o_ref.dtype)
        return carry

    lax.fori_loop(0, tb // CHUNK, chunk, 0)


def _ssd_scan(xbc, zd, a_log_row, dt_bias_row, d_skip_row, norm_g, *, batch, seq, tb):
    t = xbc.shape[0]
    nt = seq // tb
    return pl.pallas_call(
        functools.partial(_ssd_scan_kernel, tb=tb),
        grid=(batch, nt),
        in_specs=[pl.BlockSpec((tb, SSD_CONV_DIM), lambda b, i: (b * nt + i, 0)),
                  pl.BlockSpec((tb, SSD_INNER + LANES), lambda b, i: (b * nt + i, 0)),
                  _resident((1, LANES)), _resident((1, LANES)),
                  _resident((1, SSD_INNER)), _resident((1, SSD_INNER))],
        out_specs=pl.BlockSpec((tb, SSD_INNER), lambda b, i: (b * nt + i, 0)),
        out_shape=jax.ShapeDtypeStruct((t, SSD_INNER), BF16),
        scratch_shapes=[pltpu.VMEM((SSD_GROUPS, SSD_STATE, SSD_HPG * SSD_HEADDIM), F32)],
        compiler_params=_cparams("parallel", "arbitrary"),
        name="ssd_scan",
    )(xbc, zd, a_log_row, dt_bias_row, d_skip_row, norm_g)


def _rw_proj_kernel(x_ref, xh_ref, gpre_ref, mu_ref, wrkv_ref, w0_ref, w1_ref, w2_ref, a0_ref, a1_ref, a2_ref,
                    g1_ref, g2_ref, kk_ref, ka_ref,
                    r_out, lw_out, k_out, v_out, kk_out, a_out, gate_out, *, tm, tiles_per_seq):
    i = pl.program_id(0)
    g = gpre_ref[...]
    first = (i % tiles_per_seq) == 0
    x = _rms(x_ref[...], g)
    xh = _rms(jnp.where(first, 0.0, xh_ref[...]), g)
    prev = jnp.concatenate([xh, x], axis=0)[HALO - 1:HALO - 1 + tm]
    xx = prev - x
    mu = mu_ref[...]
    xr, xw, xk, xv, xa, xg = [(x + xx * mu[j:j + 1]) for j in range(6)]
    r = _mm(xr, wrkv_ref[0])
    k = _mm(xk, wrkv_ref[1])
    v = _mm(xv, wrkv_ref[2])
    w = -_softplus(-(w0_ref[...] + _mm(jnp.tanh(_mm(xw, w1_ref[...])), w2_ref[...]))) - 0.5
    a = _sigmoid(a0_ref[...] + _mm(_mm(xa, a1_ref[...]), a2_ref[...]))
    gate = _mm(_sigmoid(_mm(xg, g1_ref[...])), g2_ref[...])
    r_out[...] = r
    lw_out[...] = -jnp.exp(w)
    k_out[...] = k * (1.0 + (a - 1.0) * ka_ref[...])
    v_out[...] = v
    kk_out[...] = k * kk_ref[...]
    a_out[...] = a
    gate_out[...] = gate


def _rw_proj(h, g_pre, mu, w_rkv, w0, w1, w2, a0, a1, a2, g1, g2, k_k, k_a, *, seq, tm):
    t, d = h.shape
    hb = tm // HALO
    row = lambda: _resident((1, d))
    tok = pl.BlockSpec((tm, d), lambda i: (i, 0))
    out = jax.ShapeDtypeStruct((t, d), F32)
    return pl.pallas_call(
        functools.partial(_rw_proj_kernel, tm=tm, tiles_per_seq=seq // tm),
        grid=(t // tm,),
        in_specs=[tok, pl.BlockSpec((HALO, d), lambda i: (jnp.maximum(i * hb - 1, 0), 0)),
                  row(), _resident(mu.shape), _resident(w_rkv.shape), row(), _resident(w1.shape),
                  _resident(w2.shape), row(), _resident(a1.shape), _resident(a2.shape),
                  _resident(g1.shape), _resident(g2.shape), row(), row()],
        out_specs=[tok] * 7,
        out_shape=[out] * 7,
        compiler_params=_cparams("parallel"),
        name="rw_proj",
    )(h, h, g_pre, mu, w_rkv, w0, w1, w2, a0, a1, a2, g1, g2, k_k, k_a)


def _rw_scan_kernel(r_ref, lw_ref, k_ref, v_ref, kk_ref, a_ref, gate_ref, rk_ref, lng_ref, lnb_ref,
                    o_ref, s_ref, *, tb):
    @pl.when(pl.program_id(1) == 0)
    def _():
        s_ref[...] = jnp.zeros_like(s_ref)

    row, col = _tri_masks(CHUNK)
    incl = col <= row
    strict = col < row
    lower_ones = incl.astype(F32)
    n = RW_HEAD

    def chunk(c, carry):
        r0 = pl.multiple_of(c * CHUNK, CHUNK)
        rows = pl.ds(r0, CHUNK)
        def heads(t):
            return jnp.stack([t[:, h * n:(h + 1) * n] for h in range(RW_HEADS)])

        lw = lw_ref[rows, :]
        wc = _cumsum_rows(lower_ones, lw)
        e_inv = jnp.exp(-wc)
        r = r_ref[rows, :]
        k = k_ref[rows, :]
        vh = heads(v_ref[rows, :])
        kk = heads(kk_ref[rows, :])
        kk = kk * lax.rsqrt(jnp.sum(kk * kk, axis=-1, keepdims=True) + 1e-6)
        rt = heads(r * jnp.exp(wc))
        kt = heads(k * e_inv)
        at = -kk * heads(jnp.exp(wc - lw))
        bt = kk * heads(a_ref[rows, :] * e_inv)
        state = s_ref[...]
        a_ab = jnp.where(strict, _bmm(at, bt, _NT), 0.0)
        a_ak = jnp.where(strict, _bmm(at, kt, _NT), 0.0)
        t_inv = _unit_lower_inverse(-a_ab, row, col)
        u = _bmm(t_inv, _bmm(at, state, _NT) + _bmm(a_ak, vh))
        y = (_bmm(rt, state, _NT) + _bmm(jnp.where(incl, _bmm(rt, bt, _NT), 0.0), u)
             + _bmm(jnp.where(incl, _bmm(rt, kt, _NT), 0.0), vh))
        s_ref[...] = (state + _bmm(u, bt, _TN) + _bmm(vh, kt, _TN)) * heads(jnp.exp(wc[CHUNK - 1:CHUNK, :]))
        yc = y - jnp.mean(y, axis=-1, keepdims=True)
        yn = yc * lax.rsqrt(jnp.mean(yc * yc, axis=-1, keepdims=True) + RW_GN_EPS)
        bonus = jnp.sum(heads(r * k * rk_ref[...]), axis=-1, keepdims=True) * vh
        flat = lambda t: jnp.concatenate([t[h] for h in range(RW_HEADS)], axis=-1)
        out = (flat(yn) * lng_ref[...] + lnb_ref[...] + flat(bonus)) * gate_ref[rows, :]
        o_ref[rows, :] = out.astype(o_ref.dtype)
        return carry

    lax.fori_loop(0, tb // CHUNK, chunk, 0, unroll=2)


def _rw_scan(r, lw, k, v, kk, a, gate, r_k, ln_g, ln_b, *, batch, seq, tb):
    t, d = r.shape
    nt = seq // tb
    tok = pl.BlockSpec((tb, d), lambda bi, i: (bi * nt + i, 0))
    return pl.pallas_call(
        functools.partial(_rw_scan_kernel, tb=tb),
        grid=(batch, nt),
        in_specs=[tok] * 7 + [_resident((1, d))] * 3,
        out_specs=tok,
        out_shape=jax.ShapeDtypeStruct((t, d), BF16),
        scratch_shapes=[pltpu.VMEM((RW_HEADS, RW_HEAD, RW_HEAD), F32)],
        compiler_params=_cparams("parallel", "arbitrary"),
        name="rw_scan",
    )(r, lw, k, v, kk, a, gate, r_k, ln_g, ln_b)


def _row(v):
    return v.reshape(1, -1).astype(F32)


def _lane_row(v, offset):
    return jnp.zeros((1, LANES), F32).at[0, offset:offset + v.shape[0]].set(v.astype(F32))


def _pad_cols(w, n):
    return jnp.pad(w, ((0, 0), (0, n - w.shape[1])))


def _tile(seq, want):
    return min(seq, want)


def _deltanet_layer(h, g_pre, g_post, w_in, conv_w, a_log, dt_bias, norm_g, w_out, *, batch, seq):
    w_conv = w_in[:, :DN_CONV_DIM].astype(BF16)
    w_rest = _pad_cols(w_in[:, DN_CONV_DIM:], DN_VAL + LANES).astype(BF16)
    qkv = _norm_proj_conv(h, g_pre, w_conv, conv_w, jnp.zeros((1, DN_CONV_DIM), F32),
                          seq=seq, tm=_tile(seq, 512))
    zb = _norm_proj(h, g_pre, w_rest, tm=_tile(seq, 512))
    o = _dn_scan(qkv, zb, _lane_row(a_log, DN_V_HEADS), _lane_row(dt_bias, DN_V_HEADS), _row(norm_g),
                 batch=batch, seq=seq, tb=_tile(seq, 256))
    return _proj_post(o, h, w_out.astype(BF16), g_post, tm=_tile(seq, 512))


def _ssd_layer(h, g_pre, g_post, w_in, conv_w, conv_b, a_log, dt_bias, d_skip, norm_g, w_out, *, batch, seq):
    w_conv = w_in[:, SSD_INNER:SSD_INNER + SSD_CONV_DIM].astype(BF16)
    w_rest = _pad_cols(jnp.concatenate([w_in[:, :SSD_INNER], w_in[:, SSD_INNER + SSD_CONV_DIM:]], axis=1),
                       SSD_INNER + LANES).astype(BF16)
    xbc = _norm_proj_conv(h, g_pre, w_conv, conv_w, _row(conv_b), seq=seq, tm=_tile(seq, 512))
    zd = _norm_proj(h, g_pre, w_rest, tm=_tile(seq, 512))
    y = _ssd_scan(xbc, zd, _lane_row(a_log, 0), _lane_row(dt_bias, 0),
                  _row(jnp.repeat(d_skip, SSD_HEADDIM)), _row(norm_g), batch=batch, seq=seq, tb=_tile(seq, 256))
    return _proj_post(y, h, w_out.astype(BF16), g_post, tm=_tile(seq, 512))


def _rwkv_layer(h, g_pre, g_post, mu, w_rkv, w0, w1, w2, a0, a1, a2, g1, g2, k_k, k_a, r_k, ln_g, ln_b, w_out,
                *, batch, seq):
    bf = lambda w: w.astype(BF16)
    outs = _rw_proj(h, g_pre, mu.astype(F32), bf(w_rkv), _row(w0), bf(w1), bf(w2), _row(a0), bf(a1), bf(a2),
                    bf(g1), bf(g2), _row(k_k), _row(k_a), seq=seq, tm=_tile(seq, 256))
    y = _rw_scan(*outs, _row(r_k), _row(ln_g), _row(ln_b), batch=batch, seq=seq, tb=_tile(seq, 256))
    return _proj_post(y, h, bf(w_out), g_post, tm=_tile(seq, 512))


def kernel(x, mem, sandwich_g, ffn_w_in, ffn_w_out, mem_norm_g, xa_w_q, xa_w_kv, xa_w_o, dn_w_in, dn_conv_w, dn_a_log, dn_dt_bias, dn_norm_g, dn_w_out, ssd_w_in, ssd_conv_w, ssd_conv_b, ssd_a_log, ssd_dt_bias, ssd_d, ssd_norm_g, ssd_w_out, rw_mu, rw_w_rkv, rw_w0, rw_w1, rw_w2, rw_a0, rw_a1, rw_a2, rw_g1, rw_g2, rw_k_k, rw_k_a, rw_r_k, rw_ln_g, rw_ln_b, rw_w_out):
    batch, seq, d = x.shape
    n_mem = mem.shape[1]
    h = x.reshape(batch * seq, d)
    mem2 = mem.reshape(batch * n_mem, d)
    tm_ffn = _tile(seq, 512)

    def ffn(h, l, which, g_pre, g_post):
        w_in = ffn_w_in[l, which].astype(BF16)
        return _ffn(h, _row(g_pre), _row(g_post), w_in[:, :D_FF], w_in[:, D_FF:],
                    ffn_w_out[l, which].astype(BF16), tm=tm_ffn)

    for l in range(DEPTH):
        g = sandwich_g[l]
        kind = l % N_MIXERS
        j = l // N_MIXERS
        h = ffn(h, l, 0, g[0, 0], g[0, 1])
        gp, gq = _row(g[1, 0]), _row(g[1, 1])
        if kind == 0:
            h = _deltanet_layer(h, gp, gq, dn_w_in[j], dn_conv_w[j], dn_a_log[j], dn_dt_bias[j], dn_norm_g[j],
                                dn_w_out[j], batch=batch, seq=seq)
        elif kind == 1:
            h = _ssd_layer(h, gp, gq, ssd_w_in[j], ssd_conv_w[j], ssd_conv_b[j], ssd_a_log[j], ssd_dt_bias[j],
                           ssd_d[j], ssd_norm_g[j], ssd_w_out[j], batch=batch, seq=seq)
        else:
            h = _rwkv_layer(h, gp, gq, rw_mu[j], rw_w_rkv[j], rw_w0[j], rw_w1[j], rw_w2[j], rw_a0[j], rw_a1[j],
                            rw_a2[j], rw_g1[j], rw_g2[j], rw_k_k[j], rw_k_a[j], rw_r_k[j].reshape(-1),
                            rw_ln_g[j], rw_ln_b[j], rw_w_out[j], batch=batch, seq=seq)
        kv = _norm_proj(mem2, _row(mem_norm_g[l]), xa_w_kv[l].astype(BF16), tm=n_mem, out_dtype=BF16)
        h = _xattn(h, kv, _row(g[2, 0]), _row(g[2, 1]), xa_w_q[l].astype(BF16), xa_w_o[l].astype(BF16),
                   seq=seq, tm=_tile(seq, 512))
        h = ffn(h, l, 1, g[3, 0], g[3, 1])
    return h.reshape(batch, seq, d)
```

```python
import functools
import math

import jax
import jax.numpy as jnp
from jax import lax
from jax.experimental import pallas as pl
from jax.experimental.pallas import tpu as pltpu

F32 = jnp.float32
BF16 = jnp.bfloat16

D_MODEL = 1024
DEPTH = 4
N_MIXERS = 3
CHUNK = 64
CONV_W = 4
RMS_EPS = 1e-6
MACARON_W = 0.5
D_FF = 2816

DN_QK_HEADS = 8
DN_V_HEADS = 16
DN_DK = 128
DN_DV = 128
DN_Q = DN_QK_HEADS * DN_DK
DN_VAL = DN_V_HEADS * DN_DV
DN_CONV_DIM = 2 * DN_Q + DN_VAL

SSD_INNER = 2 * D_MODEL
SSD_HEADDIM = 64
SSD_HEADS = SSD_INNER // SSD_HEADDIM
SSD_GROUPS = 8
SSD_HPG = SSD_HEADS // SSD_GROUPS
SSD_STATE = 128
SSD_BC = SSD_GROUPS * SSD_STATE
SSD_CONV_DIM = SSD_INNER + 2 * SSD_BC

RW_HEAD = 64
RW_HEADS = D_MODEL // RW_HEAD
RW_GN_EPS = 64e-5

XA_HEADS = 4
XA_DH = D_MODEL // XA_HEADS

LANES = 128
HALO = 16
VMEM_LIMIT = 56 * 1024 * 1024
HI = lax.Precision.HIGHEST


def _cparams(*sem):
    return pltpu.CompilerParams(dimension_semantics=sem, vmem_limit_bytes=VMEM_LIMIT)


def _resident(shape):
    nd = len(shape)
    return pl.BlockSpec(shape, lambda *_: (0,) * nd, pipeline_mode=pl.Buffered(1))


def _rms(x, g, eps=RMS_EPS):
    return x * lax.rsqrt(jnp.mean(x * x, axis=-1, keepdims=True) + eps) * g


def _sigmoid(x):
    return 0.5 * jnp.tanh(0.5 * x) + 0.5


def _silu(x):
    return x * _sigmoid(x)


def _softplus(x):
    return jnp.maximum(x, 0.0) + jnp.log1p(jnp.exp(-jnp.abs(x)))


def _mm(a, b):
    return jnp.dot(a.astype(BF16), b.astype(BF16), preferred_element_type=F32)


def _mm_nt(a, b):
    return lax.dot_general(a.astype(BF16), b.astype(BF16), (((1,), (1,)), ((), ())),
                           preferred_element_type=F32)


def _mm_tn(a, b):
    return lax.dot_general(a.astype(BF16), b.astype(BF16), (((0,), (0,)), ((), ())),
                           preferred_element_type=F32)


def _mm_hi(a, b):
    return jnp.dot(a, b, preferred_element_type=F32, precision=HI)


def _tri_masks(n):
    row = lax.broadcasted_iota(jnp.int32, (n, n), 0)
    col = lax.broadcasted_iota(jnp.int32, (n, n), 1)
    return row, col


_NN = ((2,), (1,))
_NT = ((2,), (2,))
_TN = ((1,), (1,))


def _bmm(a, b, dims=_NN):
    return lax.dot_general(a.astype(BF16), b.astype(BF16), (dims, ((0,), (0,))), preferred_element_type=F32)


def _unit_lower_inverse(a, row, col):
    n = a.shape[-1]
    eye = (row == col).astype(F32)
    x = eye - jnp.where((row >> 1) == (col >> 1), a, 0.0)
    b = 2
    while b < n:
        sh = b.bit_length() - 1
        m = ((row >> (sh + 1)) == (col >> (sh + 1))) & (((row >> sh) & 1) == 1) & (((col >> sh) & 1) == 0)
        lm = jnp.where(m, a, 0.0)
        x = x - _bmm(_bmm(x, lm), x)
        b *= 2
    return x


def _cumsum_rows(lower_ones, v):
    return _mm_hi(lower_ones, v)


def _norm_proj_kernel(x_ref, g_ref, w_ref, o_ref):
    xn = _rms(x_ref[...], g_ref[...]).astype(BF16)
    o_ref[...] = jnp.dot(xn, w_ref[...], preferred_element_type=F32).astype(o_ref.dtype)


def _norm_proj(x, g, w, *, tm, out_dtype=F32):
    t, d = x.shape
    n = w.shape[1]
    return pl.pallas_call(
        _norm_proj_kernel,
        grid=(t // tm,),
        in_specs=[pl.BlockSpec((tm, d), lambda i: (i, 0)), _resident((1, d)), _resident((d, n))],
        out_specs=pl.BlockSpec((tm, n), lambda i: (i, 0)),
        out_shape=jax.ShapeDtypeStruct((t, n), out_dtype),
        compiler_params=_cparams("parallel"),
        name="norm_proj",
    )(x, g, w)


def _norm_proj_conv_kernel(x_ref, xh_ref, g_ref, w_ref, cw_ref, cb_ref, o_ref, p_ref, *, tm, tn, tiles_per_seq):
    i = pl.program_id(0)
    g = g_ref[...]
    first = (i % tiles_per_seq) == 0
    xh = jnp.where(first, 0.0, xh_ref[...])
    xn = jnp.concatenate([_rms(xh, g), _rms(x_ref[...], g)], axis=0).astype(BF16)
    n = w_ref.shape[1]
    for c in range(n // tn):
        sl = slice(c * tn, (c + 1) * tn)
        p_ref[c % 2] = jnp.dot(xn, w_ref[:, sl], preferred_element_type=F32)
        cw = cw_ref[:, sl]
        y = cb_ref[:, sl]
        for j in range(CONV_W):
            y = y + cw[j:j + 1] * p_ref[c % 2, pl.ds(HALO - (CONV_W - 1) + j, tm), :]
        o_ref[:, sl] = _silu(y)


def _norm_proj_conv(x, g, w, conv_w, conv_b, *, seq, tm, tn=512):
    t, d = x.shape
    n = w.shape[1]
    hb = tm // HALO
    kern = functools.partial(_norm_proj_conv_kernel, tm=tm, tn=tn, tiles_per_seq=seq // tm)
    return pl.pallas_call(
        kern,
        grid=(t // tm,),
        in_specs=[pl.BlockSpec((tm, d), lambda i: (i, 0)),
                  pl.BlockSpec((HALO, d), lambda i: (jnp.maximum(i * hb - 1, 0), 0)),
                  _resident((1, d)), _resident((d, n)), _resident((CONV_W, n)), _resident((1, n))],
        out_specs=pl.BlockSpec((tm, n), lambda i: (i, 0)),
        out_shape=jax.ShapeDtypeStruct((t, n), F32),
        scratch_shapes=[pltpu.VMEM((2, HALO + tm, tn), F32)],
        compiler_params=_cparams("parallel"),
        name="norm_proj_conv",
    )(x, x, g, w, conv_w, conv_b)


def _proj_post_kernel(a_ref, h_ref, w_ref, g_ref, o_ref, *, weight):
    u = jnp.dot(a_ref[...], w_ref[...], preferred_element_type=F32)
    o_ref[...] = h_ref[...] + weight * _rms(u, g_ref[...])


def _proj_post(a, h, w, g_post, *, tm, weight=1.0):
    t, k = a.shape
    d = w.shape[1]
    return pl.pallas_call(
        functools.partial(_proj_post_kernel, weight=weight),
        grid=(t // tm,),
        in_specs=[pl.BlockSpec((tm, k), lambda i: (i, 0)), pl.BlockSpec((tm, d), lambda i: (i, 0)),
                  _resident((k, d)), _resident((1, d))],
        out_specs=pl.BlockSpec((tm, d), lambda i: (i, 0)),
        out_shape=jax.ShapeDtypeStruct((t, d), F32),
        compiler_params=_cparams("parallel"),
        name="proj_post",
    )(a, h, w, g_post)


def _ffn_kernel(h_ref, gpre_ref, gpost_ref, wg_ref, wu_ref, wo_ref, o_ref, acc_ref, *, tf):
    h = h_ref[...]
    xn = _rms(h, gpre_ref[...]).astype(BF16)
    nf = wg_ref.shape[1] // tf
    for c in range(nf):
        sl = slice(c * tf, (c + 1) * tf)
        gate = jnp.dot(xn, wg_ref[:, sl], preferred_element_type=F32)
        up = jnp.dot(xn, wu_ref[:, sl], preferred_element_type=F32)
        act = (_silu(gate) * up).astype(BF16)
        part = jnp.dot(act, wo_ref[sl, :], preferred_element_type=F32)
        if c == 0:
            acc_ref[...] = part
        else:
            acc_ref[...] += part
    o_ref[...] = h + MACARON_W * _rms(acc_ref[...], gpost_ref[...])


def _ffn(h, g_pre, g_post, w_gate, w_up, w_out, *, tm, tf=256):
    t, d = h.shape
    f = w_gate.shape[1]
    return pl.pallas_call(
        functools.partial(_ffn_kernel, tf=tf),
        grid=(t // tm,),
        in_specs=[pl.BlockSpec((tm, d), lambda i: (i, 0)), _resident((1, d)), _resident((1, d)),
                  _resident((d, f)), _resident((d, f)), _resident((f, d))],
        out_specs=pl.BlockSpec((tm, d), lambda i: (i, 0)),
        out_shape=jax.ShapeDtypeStruct((t, d), F32),
        scratch_shapes=[pltpu.VMEM((tm, d), F32)],
        compiler_params=_cparams("parallel"),
        name="ffn",
    )(h, g_pre, g_post, w_gate, w_up, w_out)


def _xattn_kernel(h_ref, kv_ref, gpre_ref, gpost_ref, wq_ref, wo_ref, o_ref):
    h = h_ref[...]
    d = h.shape[1]
    xn = _rms(h, gpre_ref[...]).astype(BF16)
    q = jnp.dot(xn, wq_ref[...], preferred_element_type=F32) * XA_DH ** -0.5
    outs = []
    for hh in range(XA_HEADS):
        sl = slice(hh * XA_DH, (hh + 1) * XA_DH)
        k = kv_ref[:, sl]
        v = kv_ref[:, d + hh * XA_DH:d + (hh + 1) * XA_DH]
        s = _mm_nt(q[:, sl], k)
        s = s - jnp.max(s, axis=-1, keepdims=True)
        e = jnp.exp(s)
        p = e / jnp.sum(e, axis=-1, keepdims=True)
        outs.append(_mm(p, v))
    o = jnp.concatenate(outs, axis=-1).astype(BF16)
    u = jnp.dot(o, wo_ref[...], preferred_element_type=F32)
    o_ref[...] = h + _rms(u, gpost_ref[...])


def _xattn(h, kv, g_pre, g_post, w_q, w_o, *, seq, tm):
    t, d = h.shape
    n_mem = kv.shape[0] // (t // seq)
    tiles_per_seq = seq // tm
    return pl.pallas_call(
        _xattn_kernel,
        grid=(t // tm,),
        in_specs=[pl.BlockSpec((tm, d), lambda i: (i, 0)),
                  pl.BlockSpec((n_mem, 2 * d), lambda i: (i // tiles_per_seq, 0)),
                  _resident((1, d)), _resident((1, d)), _resident((d, d)), _resident((d, d))],
        out_specs=pl.BlockSpec((tm, d), lambda i: (i, 0)),
        out_shape=jax.ShapeDtypeStruct((t, d), F32),
        compiler_params=_cparams("parallel"),
        name="xattn",
    )(h, kv, g_pre, g_post, w_q, w_o)


def _dn_scan_kernel(qkv_ref, zb_ref, alog_ref, dtb_ref, ng_ref, o_ref, s_ref, *, tb):
    @pl.when(pl.program_id(1) == 0)
    def _():
        s_ref[...] = jnp.zeros_like(s_ref)

    row, col = _tri_masks(CHUNK)
    incl = col <= row
    strict = col < row
    lower_ones = incl.astype(F32)
    neg_a = -jnp.exp(alog_ref[...])
    dtb = dtb_ref[...]
    ng = ng_ref[...]

    def chunk(c, carry):
        r0 = pl.multiple_of(c * CHUNK, CHUNK)
        rows = pl.ds(r0, CHUNK)
        ba = zb_ref[rows, DN_VAL:DN_VAL + LANES]
        beta_all = _sigmoid(ba)
        g_all = neg_a * _softplus(ba + dtb)
        gc = _cumsum_rows(lower_ones, g_all)
        gct = gc.T
        eg_all = jnp.exp(gc)
        glast = gc[CHUNK - 1:CHUNK, :]
        eend_all = jnp.exp(glast - gc)
        elast_all = jnp.exp(glast)
        qs, ks = [], []
        for hq in range(DN_QK_HEADS):
            q = qkv_ref[rows, hq * DN_DK:(hq + 1) * DN_DK]
            k = qkv_ref[rows, DN_Q + hq * DN_DK:DN_Q + (hq + 1) * DN_DK]
            qs.append(q * lax.rsqrt(jnp.sum(q * q, axis=-1, keepdims=True) + 1e-6) * DN_DK ** -0.5)
            ks.append(k * lax.rsqrt(jnp.sum(k * k, axis=-1, keepdims=True) + 1e-6))
        k8 = jnp.stack(ks)
        qkk = _bmm(jnp.stack([jnp.concatenate([q, k], axis=0) for q, k in zip(qs, ks)]), k8, _NT)
        qk8, kk8 = qkk[:, :CHUNK], qkk[:, CHUNK:]
        a_l, att_l, rhs_l, qe_l, kend_l, elast_l = [], [], [], [], [], []
        for h in range(DN_V_HEADS):
            hq = h // (DN_V_HEADS // DN_QK_HEADS)
            gl = DN_V_HEADS + h
            beta = beta_all[:, h:h + 1]
            eg = eg_all[:, gl:gl + 1]
            decay = jnp.where(incl, jnp.exp(gc[:, gl:gl + 1] - gct[gl:gl + 1, :]), 0.0)
            a_l.append(jnp.where(strict, kk8[hq] * decay * beta, 0.0))
            att_l.append(qk8[hq] * decay)
            v = qkv_ref[rows, 2 * DN_Q + h * DN_DV:2 * DN_Q + (h + 1) * DN_DV]
            rhs_l.append(jnp.concatenate([v * beta, ks[hq] * (beta * eg)], axis=-1))
            qe_l.append(qs[hq] * eg)
            kend_l.append(ks[hq] * eend_all[:, gl:gl + 1])
            elast_l.append(elast_all[:, gl:gl + 1])
        t_inv = _unit_lower_inverse(jnp.stack(a_l), row, col)
        sol = _bmm(t_inv, jnp.stack(rhs_l))
        u, wk = sol[:, :, :DN_DV], sol[:, :, DN_DV:]
        state = s_ref[...]
        ws = _bmm(jnp.concatenate([wk, jnp.stack(qe_l)], axis=1), state)
        v_new = u - ws[:, :CHUNK]
        o = ws[:, CHUNK:] + _bmm(jnp.stack(att_l), v_new)
        s_ref[...] = state * jnp.stack(elast_l) + _bmm(jnp.stack(kend_l), v_new, _TN)
        for h in range(DN_V_HEADS):
            vsl = slice(h * DN_DV, (h + 1) * DN_DV)
            z = zb_ref[rows, vsl]
            o_ref[rows, vsl] = (_rms(o[h], ng) * _silu(z)).astype(o_ref.dtype)
        return carry

    lax.fori_loop(0, tb // CHUNK, chunk, 0, unroll=2)


def _dn_scan(qkv, zb, a_log_row, dt_bias_row, norm_g, *, batch, seq, tb):
    t = qkv.shape[0]
    nt = seq // tb
    return pl.pallas_call(
        functools.partial(_dn_scan_kernel, tb=tb),
        grid=(batch, nt),
        in_specs=[pl.BlockSpec((tb, DN_CONV_DIM), lambda b, i: (b * nt + i, 0)),
                  pl.BlockSpec((tb, DN_VAL + LANES), lambda b, i: (b * nt + i, 0)),
                  _resident((1, LANES)), _resident((1, LANES)), _resident((1, DN_DV))],
        out_specs=pl.BlockSpec((tb, DN_VAL), lambda b, i: (b * nt + i, 0)),
        out_shape=jax.ShapeDtypeStruct((t, DN_VAL), BF16),
        scratch_shapes=[pltpu.VMEM((DN_V_HEADS, DN_DK, DN_DV), F32)],
        compiler_params=_cparams("parallel", "arbitrary"),
        name="dn_scan",
    )(qkv, zb, a_log_row, dt_bias_row, norm_g)


def _ssd_scan_kernel(xbc_ref, zd_ref, alog_ref, dtb_ref, dskip_ref, ng_ref, o_ref, s_ref, *, tb):
    @pl.when(pl.program_id(1) == 0)
    def _():
        s_ref[...] = jnp.zeros_like(s_ref)

    row, col = _tri_masks(CHUNK)
    lower_ones = (col <= row).astype(F32)
    neg_a = -jnp.exp(alog_ref[...])
    dtb = dtb_ref[...]
    gsz = SSD_HPG * SSD_HEADDIM
    lane = lax.broadcasted_iota(jnp.int32, (CHUNK, LANES), 1)
    pair = lane // SSD_HEADDIM
    rowg = lax.broadcasted_iota(jnp.int32, (CHUNK, gsz), 0)
    laneg = lax.broadcasted_iota(jnp.int32, (CHUNK, gsz), 1)
    incl_g = (laneg & (SSD_HEADDIM - 1)) <= rowg
    head_g = laneg // SSD_HEADDIM

    def spread(cols, g):
        halves = [jnp.take_along_axis(cols, pair + (g * SSD_HPG + 2 * p), axis=1) for p in range(2)]
        return jnp.concatenate(halves, axis=1)

    def chunk(c, carry):
        r0 = pl.multiple_of(c * CHUNK, CHUNK)
        rows = pl.ds(r0, CHUNK)
        dt_all = _softplus(zd_ref[rows, SSD_INNER:SSD_INNER + LANES] + dtb)
        ac = _cumsum_rows(lower_ones, dt_all * neg_a)
        act = ac.T
        bm = jnp.stack([xbc_ref[rows, SSD_INNER + g * SSD_STATE:SSD_INNER + (g + 1) * SSD_STATE]
                        for g in range(SSD_GROUPS)])
        cm = jnp.stack([xbc_ref[rows, SSD_INNER + SSD_BC + g * SSD_STATE:SSD_INNER + SSD_BC + (g + 1) * SSD_STATE]
                        for g in range(SSD_GROUPS)])
        cb = _bmm(cm, jnp.concatenate([bm] * SSD_HPG, axis=1), _NT)
        state = s_ref[...]
        y_state = _bmm(cm, state)
        w_l, blk_l, xe_l, el_l, ea_l = [], [], [], [], []
        for g in range(SSD_GROUPS):
            xg = xbc_ref[rows, g * gsz:(g + 1) * gsz]
            ac_col = spread(ac, g)
            ac_row = jnp.concatenate([act[g * SSD_HPG + e:g * SSD_HPG + e + 1, :] for e in range(SSD_HPG)], axis=1)
            a_last = ac_col[CHUNK - 1:CHUNK, :]
            xdt = xg * spread(dt_all, g)
            w_l.append(cb[g] * jnp.where(incl_g, jnp.exp(ac_col - ac_row), 0.0))
            blk_l.append(jnp.concatenate([jnp.where(head_g == e, xdt, 0.0) for e in range(SSD_HPG)], axis=0))
            xe_l.append(xdt * jnp.exp(a_last - ac_col))
            el_l.append(jnp.exp(a_last))
            ea_l.append(jnp.exp(ac_col))
        y_intra = _bmm(jnp.stack(w_l), jnp.stack(blk_l))
        s_ref[...] = state * jnp.stack(el_l) + _bmm(bm, jnp.stack(xe_l), _TN)
        for g in range(SSD_GROUPS):
            gsl = slice(g * gsz, (g + 1) * gsz)
            y = y_intra[g] + y_state[g] * ea_l[g]
            y = y + xbc_ref[rows, gsl] * dskip_ref[:, gsl]
            yz = y * _silu(zd_ref[rows, gsl])
            o_ref[rows, gsl] = _rms(yz, ng_ref[:, gsl]).astype(o_ref.dtype)
        return carry

    lax.fori_loop(0, tb // CHUNK, chunk, 0, unroll=2)


def _ssd_scan(xbc, zd, a_log_row, dt_bias_row, d_skip_row, norm_g, *, batch, seq, tb):
    t = xbc.shape[0]
    nt = seq // tb
    return pl.pallas_call(
        functools.partial(_ssd_scan_kernel, tb=tb),
        grid=(batch, nt),
        in_specs=[pl.BlockSpec((tb, SSD_CONV_DIM), lambda b, i: (b * nt + i, 0)),
                  pl.BlockSpec((tb, SSD_INNER + LANES), lambda b, i: (b * nt + i, 0)),
                  _resident((1, LANES)), _resident((1, LANES)),
                  _resident((1, SSD_INNER)), _resident((1, SSD_INNER))],
        out_specs=pl.BlockSpec((tb, SSD_INNER), lambda b, i: (b * nt + i, 0)),
        out_shape=jax.ShapeDtypeStruct((t, SSD_INNER), BF16),
        scratch_shapes=[pltpu.VMEM((SSD_GROUPS, SSD_STATE, SSD_HPG * SSD_HEADDIM), F32)],
        compiler_params=_cparams("parallel", "arbitrary"),
        name="ssd_scan",
    )(xbc, zd, a_log_row, dt_bias_row, d_skip_row, norm_g)


def _rw_proj_kernel(x_ref, xh_ref, gpre_ref, mu_ref, wrkv_ref, w0_ref, w1_ref, w2_ref, a0_ref, a1_ref, a2_ref,
                    g1_ref, g2_ref, kk_ref, ka_ref,
                    r_out, lw_out, k_out, v_out, kk_out, a_out, gate_out, *, tm, tiles_per_seq):
    i = pl.program_id(0)
    g = gpre_ref[...]
    first = (i % tiles_per_seq) == 0
    x = _rms(x_ref[...], g)
    xh = _rms(jnp.where(first, 0.0, xh_ref[...]), g)
    prev = jnp.concatenate([xh, x], axis=0)[HALO - 1:HALO - 1 + tm]
    xx = prev - x
    mu = mu_ref[...]
    xr, xw, xk, xv, xa, xg = [(x + xx * mu[j:j + 1]) for j in range(6)]
    r = _mm(xr, wrkv_ref[0])
    k = _mm(xk, wrkv_ref[1])
    v = _mm(xv, wrkv_ref[2])
    w = -_softplus(-(w0_ref[...] + _mm(jnp.tanh(_mm(xw, w1_ref[...])), w2_ref[...]))) - 0.5
    a = _sigmoid(a0_ref[...] + _mm(_mm(xa, a1_ref[...]), a2_ref[...]))
    gate = _mm(_sigmoid(_mm(xg, g1_ref[...])), g2_ref[...])
    r_out[...] = r.astype(r_out.dtype)
    lw_out[...] = -jnp.exp(w)
    k_out[...] = (k * (1.0 + (a - 1.0) * ka_ref[...])).astype(k_out.dtype)
    v_out[...] = v.astype(v_out.dtype)
    kk_out[...] = (k * kk_ref[...]).astype(kk_out.dtype)
    a_out[...] = a.astype(a_out.dtype)
    gate_out[...] = gate.astype(gate_out.dtype)


def _rw_proj(h, g_pre, mu, w_rkv, w0, w1, w2, a0, a1, a2, g1, g2, k_k, k_a, *, seq, tm):
    t, d = h.shape
    hb = tm // HALO
    row = lambda: _resident((1, d))
    tok = pl.BlockSpec((tm, d), lambda i: (i, 0))
    outs = [jax.ShapeDtypeStruct((t, d), F32 if j == 1 else BF16) for j in range(7)]
    return pl.pallas_call(
        functools.partial(_rw_proj_kernel, tm=tm, tiles_per_seq=seq // tm),
        grid=(t // tm,),
        in_specs=[tok, pl.BlockSpec((HALO, d), lambda i: (jnp.maximum(i * hb - 1, 0), 0)),
                  row(), _resident(mu.shape), _resident(w_rkv.shape), row(), _resident(w1.shape),
                  _resident(w2.shape), row(), _resident(a1.shape), _resident(a2.shape),
                  _resident(g1.shape), _resident(g2.shape), row(), row()],
        out_specs=[tok] * 7,
        out_shape=outs,
        compiler_params=_cparams("parallel"),
        name="rw_proj",
    )(h, h, g_pre, mu, w_rkv, w0, w1, w2, a0, a1, a2, g1, g2, k_k, k_a)


def _rw_scan_kernel(r_ref, lw_ref, k_ref, v_ref, kk_ref, a_ref, gate_ref, rk_ref, lng_ref, lnb_ref,
                    o_ref, s_ref, *, tb):
    @pl.when(pl.program_id(1) == 0)
    def _():
        s_ref[...] = jnp.zeros_like(s_ref)

    row, col = _tri_masks(CHUNK)
    incl = col <= row
    strict = col < row
    lower_ones = incl.astype(F32)
    n = RW_HEAD
    row2 = lax.broadcasted_iota(jnp.int32, (CHUNK, 2 * CHUNK), 0)
    col2 = lax.broadcasted_iota(jnp.int32, (CHUNK, 2 * CHUNK), 1)
    incl2 = (col2 & (CHUNK - 1)) <= row2

    def chunk(c, carry):
        r0 = pl.multiple_of(c * CHUNK, CHUNK)
        rows = pl.ds(r0, CHUNK)
        def heads(t):
            return jnp.stack([t[:, h * n:(h + 1) * n] for h in range(RW_HEADS)])

        lw = lw_ref[rows, :]
        wc = _cumsum_rows(lower_ones, lw)
        e_inv = jnp.exp(-wc)
        r = r_ref[rows, :].astype(F32)
        k = k_ref[rows, :].astype(F32)
        vh = heads(v_ref[rows, :].astype(F32))
        kk = heads(kk_ref[rows, :].astype(F32))
        kk = kk * lax.rsqrt(jnp.sum(kk * kk, axis=-1, keepdims=True) + 1e-6)
        rt = heads(r * jnp.exp(wc))
        kt = heads(k * e_inv)
        at = -kk * heads(jnp.exp(wc - lw))
        bt = kk * heads(a_ref[rows, :].astype(F32) * e_inv)
        state = s_ref[...]
        bk = jnp.concatenate([bt, kt], axis=1)
        p = _bmm(jnp.concatenate([at, rt], axis=1), jnp.concatenate([bk, state], axis=1), _NT)
        a_ab = jnp.where(strict, p[:, :CHUNK, :CHUNK], 0.0)
        a_ak = jnp.where(strict, p[:, :CHUNK, CHUNK:2 * CHUNK], 0.0)
        t_inv = _unit_lower_inverse(-a_ab, row, col)
        u = _bmm(t_inv, p[:, :CHUNK, 2 * CHUNK:] + _bmm(a_ak, vh))
        uv = jnp.concatenate([u, vh], axis=1)
        y = p[:, CHUNK:, 2 * CHUNK:] + _bmm(jnp.where(incl2, p[:, CHUNK:, :2 * CHUNK], 0.0), uv)
        s_ref[...] = (state + _bmm(uv, bk, _TN)) * heads(jnp.exp(wc[CHUNK - 1:CHUNK, :]))
        yc = y - jnp.mean(y, axis=-1, keepdims=True)
        yn = yc * lax.rsqrt(jnp.mean(yc * yc, axis=-1, keepdims=True) + RW_GN_EPS)
        bonus = jnp.sum(heads(r * k * rk_ref[...]), axis=-1, keepdims=True) * vh
        flat = lambda t: jnp.concatenate([t[h] for h in range(RW_HEADS)], axis=-1)
        out = (flat(yn) * lng_ref[...] + lnb_ref[...] + flat(bonus)) * gate_ref[rows, :].astype(F32)
        o_ref[rows, :] = out.astype(o_ref.dtype)
        return carry

    lax.fori_loop(0, tb // CHUNK, chunk, 0, unroll=2)


def _rw_scan(r, lw, k, v, kk, a, gate, r_k, ln_g, ln_b, *, batch, seq, tb):
    t, d = r.shape
    nt = seq // tb
    tok = pl.BlockSpec((tb, d), lambda bi, i: (bi * nt + i, 0))
    return pl.pallas_call(
        functools.partial(_rw_scan_kernel, tb=tb),
        grid=(batch, nt),
        in_specs=[tok] * 7 + [_resident((1, d))] * 3,
        out_specs=tok,
        out_shape=jax.ShapeDtypeStruct((t, d), BF16),
        scratch_shapes=[pltpu.VMEM((RW_HEADS, RW_HEAD, RW_HEAD), F32)],
        compiler_params=_cparams("parallel", "arbitrary"),
        name="rw_scan",
    )(r, lw, k, v, kk, a, gate, r_k, ln_g, ln_b)


def _row(v):
    return v.reshape(1, -1).astype(F32)


def _lane_row(v, offset):
    return jnp.zeros((1, LANES), F32).at[0, offset:offset + v.shape[0]].set(v.astype(F32))


def _pad_cols(w, n):
    return jnp.pad(w, ((0, 0), (0, n - w.shape[1])))


def _tile(seq, want):
    return min(seq, want)


def _deltanet_layer(h, g_pre, g_post, w_in, conv_w, a_log, dt_bias, norm_g, w_out, *, batch, seq):
    w_conv = w_in[:, :DN_CONV_DIM].astype(BF16)
    w_rest = _pad_cols(w_in[:, DN_CONV_DIM:], DN_VAL + LANES).astype(BF16)
    qkv = _norm_proj_conv(h, g_pre, w_conv, conv_w, jnp.zeros((1, DN_CONV_DIM), F32),
                          seq=seq, tm=_tile(seq, 512))
    zb = _norm_proj(h, g_pre, w_rest, tm=_tile(seq, 512))
    o = _dn_scan(qkv, zb, _lane_row(a_log, DN_V_HEADS), _lane_row(dt_bias, DN_V_HEADS), _row(norm_g),
                 batch=batch, seq=seq, tb=_tile(seq, 256))
    return _proj_post(o, h, w_out.astype(BF16), g_post, tm=_tile(seq, 512))


def _ssd_layer(h, g_pre, g_post, w_in, conv_w, conv_b, a_log, dt_bias, d_skip, norm_g, w_out, *, batch, seq):
    w_conv = w_in[:, SSD_INNER:SSD_INNER + SSD_CONV_DIM].astype(BF16)
    w_rest = _pad_cols(jnp.concatenate([w_in[:, :SSD_INNER], w_in[:, SSD_INNER + SSD_CONV_DIM:]], axis=1),
                       SSD_INNER + LANES).astype(BF16)
    xbc = _norm_proj_conv(h, g_pre, w_conv, conv_w, _row(conv_b), seq=seq, tm=_tile(seq, 512))
    zd = _norm_proj(h, g_pre, w_rest, tm=_tile(seq, 512))
    y = _ssd_scan(xbc, zd, _lane_row(a_log, 0), _lane_row(dt_bias, 0),
                  _row(jnp.repeat(d_skip, SSD_HEADDIM)), _row(norm_g), batch=batch, seq=seq, tb=_tile(seq, 256))
    return _proj_post(y, h, w_out.astype(BF16), g_post, tm=_tile(seq, 512))


def _rwkv_layer(h, g_pre, g_post, mu, w_rkv, w0, w1, w2, a0, a1, a2, g1, g2, k_k, k_a, r_k, ln_g, ln_b, w_out,
                *, batch, seq):
    bf = lambda w: w.astype(BF16)
    outs = _rw_proj(h, g_pre, mu.astype(F32), bf(w_rkv), _row(w0), bf(w1), bf(w2), _row(a0), bf(a1), bf(a2),
                    bf(g1), bf(g2), _row(k_k), _row(k_a), seq=seq, tm=_tile(seq, 256))
    y = _rw_scan(*outs, _row(r_k), _row(ln_g), _row(ln_b), batch=batch, seq=seq, tb=_tile(seq, 256))
    return _proj_post(y, h, bf(w_out), g_post, tm=_tile(seq, 512))


def kernel(x, mem, sandwich_g, ffn_w_in, ffn_w_out, mem_norm_g, xa_w_q, xa_w_kv, xa_w_o, dn_w_in, dn_conv_w, dn_a_log, dn_dt_bias, dn_norm_g, dn_w_out, ssd_w_in, ssd_conv_w, ssd_conv_b, ssd_a_log, ssd_dt_bias, ssd_d, ssd_norm_g, ssd_w_out, rw_mu, rw_w_rkv, rw_w0, rw_w1, rw_w2, rw_a0, rw_a1, rw_a2, rw_g1, rw_g2, rw_k_k, rw_k_a, rw_r_k, rw_ln_g, rw_ln_b, rw_w_out):
    batch, seq, d = x.shape
    n_mem = mem.shape[1]
    h = x.reshape(batch * seq, d)
    mem2 = mem.reshape(batch * n_mem, d)
    tm_ffn = _tile(seq, 512)

    def ffn(h, l, which, g_pre, g_post):
        w_in = ffn_w_in[l, which].astype(BF16)
        return _ffn(h, _row(g_pre), _row(g_post), w_in[:, :D_FF], w_in[:, D_FF:],
                    ffn_w_out[l, which].astype(BF16), tm=tm_ffn)

    for l in range(DEPTH):
        g = sandwich_g[l]
        kind = l % N_MIXERS
        j = l // N_MIXERS
        h = ffn(h, l, 0, g[0, 0], g[0, 1])
        gp, gq = _row(g[1, 0]), _row(g[1, 1])
        if kind == 0:
            h = _deltanet_layer(h, gp, gq, dn_w_in[j], dn_conv_w[j], dn_a_log[j], dn_dt_bias[j], dn_norm_g[j],
                                dn_w_out[j], batch=batch, seq=seq)
        elif kind == 1:
            h = _ssd_layer(h, gp, gq, ssd_w_in[j], ssd_conv_w[j], ssd_conv_b[j], ssd_a_log[j], ssd_dt_bias[j],
                           ssd_d[j], ssd_norm_g[j], ssd_w_out[j], batch=batch, seq=seq)
        else:
            h = _rwkv_layer(h, gp, gq, rw_mu[j], rw_w_rkv[j], rw_w0[j], rw_w1[j], rw_w2[j], rw_a0[j], rw_a1[j],
                            rw_a2[j], rw_g1[j], rw_g2[j], rw_k_k[j], rw_k_a[j], rw_r_k[j].reshape(-1),
                            rw_ln_g[j], rw_ln_b[j], rw_w_out[j], batch=batch, seq=seq)
        kv = _norm_proj(mem2, _row(mem_norm_g[l]), xa_w_kv[l].astype(BF16), tm=n_mem, out_dtype=BF16)
        h = _xattn(h, kv, _row(g[2, 0]), _row(g[2, 1]), xa_w_q[l].astype(BF16), xa_w_o[l].astype(BF16),
                   seq=seq, tm=_tile(seq, 512))
        h = ffn(h, l, 1, g[3, 0], g[3, 1])
    return h.reshape(batch, seq, d)
```

```python
import functools
import math

import jax
import jax.numpy as jnp
from jax import lax
from jax.experimental import pallas as pl
from jax.experimental.pallas import tpu as pltpu

F32 = jnp.float32
BF16 = jnp.bfloat16

D_MODEL = 1024
DEPTH = 4
N_MIXERS = 3
CHUNK = 64
CONV_W = 4
RMS_EPS = 1e-6
MACARON_W = 0.5
D_FF = 2816

DN_QK_HEADS = 8
DN_V_HEADS = 16
DN_DK = 128
DN_DV = 128
DN_Q = DN_QK_HEADS * DN_DK
DN_VAL = DN_V_HEADS * DN_DV
DN_CONV_DIM = 2 * DN_Q + DN_VAL

SSD_INNER = 2 * D_MODEL
SSD_HEADDIM = 64
SSD_HEADS = SSD_INNER // SSD_HEADDIM
SSD_GROUPS = 8
SSD_HPG = SSD_HEADS // SSD_GROUPS
SSD_STATE = 128
SSD_BC = SSD_GROUPS * SSD_STATE
SSD_CONV_DIM = SSD_INNER + 2 * SSD_BC

RW_HEAD = 64
RW_HEADS = D_MODEL // RW_HEAD
RW_GN_EPS = 64e-5

XA_HEADS = 4
XA_DH = D_MODEL // XA_HEADS

LANES = 128
SUBLANES = 8
HALO = 16
VMEM_LIMIT = 56 * 1024 * 1024
HI = lax.Precision.HIGHEST


def _cparams(*sem):
    return pltpu.CompilerParams(dimension_semantics=sem, vmem_limit_bytes=VMEM_LIMIT)


def _resident(shape):
    nd = len(shape)
    return pl.BlockSpec(shape, lambda *_: (0,) * nd, pipeline_mode=pl.Buffered(1))


def _rms(x, g, eps=RMS_EPS):
    return x * lax.rsqrt(jnp.mean(x * x, axis=-1, keepdims=True) + eps) * g


def _sigmoid(x):
    return 0.5 * jnp.tanh(0.5 * x) + 0.5


def _silu(x):
    return x * _sigmoid(x)


def _softplus(x):
    return jnp.maximum(x, 0.0) + jnp.log1p(jnp.exp(-jnp.abs(x)))


def _mm(a, b):
    return jnp.dot(a.astype(BF16), b.astype(BF16), preferred_element_type=F32)


def _mm_nt(a, b):
    return lax.dot_general(a.astype(BF16), b.astype(BF16), (((1,), (1,)), ((), ())),
                           preferred_element_type=F32)


def _mm_tn(a, b):
    return lax.dot_general(a.astype(BF16), b.astype(BF16), (((0,), (0,)), ((), ())),
                           preferred_element_type=F32)


def _mm_hi(a, b):
    return jnp.dot(a, b, preferred_element_type=F32, precision=HI)


def _tri_masks(n):
    row = lax.broadcasted_iota(jnp.int32, (n, n), 0)
    col = lax.broadcasted_iota(jnp.int32, (n, n), 1)
    return row, col


_NN = ((2,), (1,))
_NT = ((2,), (2,))
_TN = ((1,), (1,))


def _bmm(a, b, dims=_NN):
    return lax.dot_general(a.astype(BF16), b.astype(BF16), (dims, ((0,), (0,))), preferred_element_type=F32)


def _unit_lower_inverse(a, row, col):
    n = a.shape[-1]
    eye = (row == col).astype(F32)
    x = eye - jnp.where((row >> 1) == (col >> 1), a, 0.0)
    b = 2
    while b < n:
        sh = b.bit_length() - 1
        m = ((row >> (sh + 1)) == (col >> (sh + 1))) & (((row >> sh) & 1) == 1) & (((col >> sh) & 1) == 0)
        lm = jnp.where(m, a, 0.0)
        x = x - _bmm(_bmm(x, lm), x)
        b *= 2
    return x


def _pair_masks(n):
    row = lax.broadcasted_iota(jnp.int32, (n, 2 * n), 0)
    lane = lax.broadcasted_iota(jnp.int32, (n, 2 * n), 1)
    return row, lane & (n - 1), lane // n


def _block_diag(p, blk):
    pb = p.astype(BF16)
    keep1 = blk.astype(BF16)
    return jnp.concatenate([pb * (1 - keep1), pb * keep1], axis=1)


def _unit_lower_inverse_pairs(a, row, col, blk):
    n = a.shape[1]
    eye = (row == col).astype(F32)
    x = eye - jnp.where((row >> 1) == (col >> 1), a, 0.0)
    b = 2
    while b < n:
        sh = b.bit_length() - 1
        m = ((row >> (sh + 1)) == (col >> (sh + 1))) & (((row >> sh) & 1) == 1) & (((col >> sh) & 1) == 0)
        lm = jnp.where(m, a, 0.0)
        x = x - _bmm(_bmm(x, _block_diag(lm, blk)), _block_diag(x, blk))
        b *= 2
    return x


def _cumsum_rows(lower_ones, v):
    return _mm_hi(lower_ones, v)


def _norm_proj_kernel(x_ref, g_ref, w_ref, o_ref):
    xn = _rms(x_ref[...], g_ref[...]).astype(BF16)
    o_ref[...] = jnp.dot(xn, w_ref[...], preferred_element_type=F32).astype(o_ref.dtype)


def _norm_proj(x, g, w, *, tm, out_dtype=F32):
    t, d = x.shape
    n = w.shape[1]
    return pl.pallas_call(
        _norm_proj_kernel,
        grid=(t // tm,),
        in_specs=[pl.BlockSpec((tm, d), lambda i: (i, 0)), _resident((1, d)), _resident((d, n))],
        out_specs=pl.BlockSpec((tm, n), lambda i: (i, 0)),
        out_shape=jax.ShapeDtypeStruct((t, n), out_dtype),
        compiler_params=_cparams("parallel"),
        name="norm_proj",
    )(x, g, w)


def _norm_proj_conv_kernel(x_ref, xh_ref, g_ref, w_ref, cw_ref, cb_ref, o_ref, p_ref, *, tm, tn, tiles_per_seq):
    i = pl.program_id(0)
    g = g_ref[...]
    first = (i % tiles_per_seq) == 0
    xh = jnp.where(first, 0.0, xh_ref[...])
    xn = jnp.concatenate([_rms(xh, g), _rms(x_ref[...], g)], axis=0).astype(BF16)
    n = w_ref.shape[1]
    for c in range(n // tn):
        sl = slice(c * tn, (c + 1) * tn)
        p_ref[c % 2] = jnp.dot(xn, w_ref[:, sl], preferred_element_type=F32)
        cw = cw_ref[:, sl]
        y = cb_ref[:, sl]
        for j in range(CONV_W):
            y = y + cw[j:j + 1] * p_ref[c % 2, pl.ds(HALO - (CONV_W - 1) + j, tm), :]
        o_ref[:, sl] = _silu(y)


def _norm_proj_conv(x, g, w, conv_w, conv_b, *, seq, tm, tn=512):
    t, d = x.shape
    n = w.shape[1]
    hb = tm // HALO
    kern = functools.partial(_norm_proj_conv_kernel, tm=tm, tn=tn, tiles_per_seq=seq // tm)
    return pl.pallas_call(
        kern,
        grid=(t // tm,),
        in_specs=[pl.BlockSpec((tm, d), lambda i: (i, 0)),
                  pl.BlockSpec((HALO, d), lambda i: (jnp.maximum(i * hb - 1, 0), 0)),
                  _resident((1, d)), _resident((d, n)), _resident((CONV_W, n)), _resident((1, n))],
        out_specs=pl.BlockSpec((tm, n), lambda i: (i, 0)),
        out_shape=jax.ShapeDtypeStruct((t, n), F32),
        scratch_shapes=[pltpu.VMEM((2, HALO + tm, tn), F32)],
        compiler_params=_cparams("parallel"),
        name="norm_proj_conv",
    )(x, x, g, w, conv_w, conv_b)


def _proj_post_kernel(a_ref, h_ref, w_ref, g_ref, o_ref, *, weight):
    u = jnp.dot(a_ref[...], w_ref[...], preferred_element_type=F32)
    o_ref[...] = h_ref[...] + weight * _rms(u, g_ref[...])


def _proj_post(a, h, w, g_post, *, tm, weight=1.0):
    t, k = a.shape
    d = w.shape[1]
    return pl.pallas_call(
        functools.partial(_proj_post_kernel, weight=weight),
        grid=(t // tm,),
        in_specs=[pl.BlockSpec((tm, k), lambda i: (i, 0)), pl.BlockSpec((tm, d), lambda i: (i, 0)),
                  _resident((k, d)), _resident((1, d))],
        out_specs=pl.BlockSpec((tm, d), lambda i: (i, 0)),
        out_shape=jax.ShapeDtypeStruct((t, d), F32),
        compiler_params=_cparams("parallel"),
        name="proj_post",
    )(a, h, w, g_post)


def _ffn_kernel(h_ref, gpre_ref, gpost_ref, wg_ref, wu_ref, wo_ref, o_ref, acc_ref, *, tf):
    h = h_ref[...]
    xn = _rms(h, gpre_ref[...]).astype(BF16)
    nf = wg_ref.shape[1] // tf
    for c in range(nf):
        sl = slice(c * tf, (c + 1) * tf)
        gate = jnp.dot(xn, wg_ref[:, sl], preferred_element_type=F32)
        up = jnp.dot(xn, wu_ref[:, sl], preferred_element_type=F32)
        act = (_silu(gate) * up).astype(BF16)
        part = jnp.dot(act, wo_ref[sl, :], preferred_element_type=F32)
        if c == 0:
            acc_ref[...] = part
        else:
            acc_ref[...] += part
    o_ref[...] = h + MACARON_W * _rms(acc_ref[...], gpost_ref[...])


def _ffn(h, g_pre, g_post, w_gate, w_up, w_out, *, tm, tf=256):
    t, d = h.shape
    f = w_gate.shape[1]
    return pl.pallas_call(
        functools.partial(_ffn_kernel, tf=tf),
        grid=(t // tm,),
        in_specs=[pl.BlockSpec((tm, d), lambda i: (i, 0)), _resident((1, d)), _resident((1, d)),
                  _resident((d, f)), _resident((d, f)), _resident((f, d))],
        out_specs=pl.BlockSpec((tm, d), lambda i: (i, 0)),
        out_shape=jax.ShapeDtypeStruct((t, d), F32),
        scratch_shapes=[pltpu.VMEM((tm, d), F32)],
        compiler_params=_cparams("parallel"),
        name="ffn",
    )(h, g_pre, g_post, w_gate, w_up, w_out)


def _xattn_kernel(h_ref, kv_ref, gpre_ref, gpost_ref, wq_ref, wo_ref, o_ref):
    h = h_ref[...]
    d = h.shape[1]
    xn = _rms(h, gpre_ref[...]).astype(BF16)
    q = jnp.dot(xn, wq_ref[...], preferred_element_type=F32) * XA_DH ** -0.5
    outs = []
    for hh in range(XA_HEADS):
        sl = slice(hh * XA_DH, (hh + 1) * XA_DH)
        k = kv_ref[:, sl]
        v = kv_ref[:, d + hh * XA_DH:d + (hh + 1) * XA_DH]
        s = _mm_nt(q[:, sl], k)
        s = s - jnp.max(s, axis=-1, keepdims=True)
        e = jnp.exp(s)
        p = e / jnp.sum(e, axis=-1, keepdims=True)
        outs.append(_mm(p, v))
    o = jnp.concatenate(outs, axis=-1).astype(BF16)
    u = jnp.dot(o, wo_ref[...], preferred_element_type=F32)
    o_ref[...] = h + _rms(u, gpost_ref[...])


def _xattn(h, kv, g_pre, g_post, w_q, w_o, *, seq, tm):
    t, d = h.shape
    n_mem = kv.shape[0] // (t // seq)
    tiles_per_seq = seq // tm
    return pl.pallas_call(
        _xattn_kernel,
        grid=(t // tm,),
        in_specs=[pl.BlockSpec((tm, d), lambda i: (i, 0)),
                  pl.BlockSpec((n_mem, 2 * d), lambda i: (i // tiles_per_seq, 0)),
                  _resident((1, d)), _resident((1, d)), _resident((d, d)), _resident((d, d))],
        out_specs=pl.BlockSpec((tm, d), lambda i: (i, 0)),
        out_shape=jax.ShapeDtypeStruct((t, d), F32),
        compiler_params=_cparams("parallel"),
        name="xattn",
    )(h, kv, g_pre, g_post, w_q, w_o)


def _dn_scan_kernel(qkv_ref, zb_ref, alog_ref, dtb_ref, ng_ref, o_ref, s_ref, *, tb):
    @pl.when(pl.program_id(1) == 0)
    def _():
        s_ref[...] = jnp.zeros_like(s_ref)

    row, col = _tri_masks(CHUNK)
    lower_ones = (col <= row).astype(F32)
    rowp, colp, blk = _pair_masks(CHUNK)
    incl_p = colp <= rowp
    strict_p = colp < rowp
    rep = DN_V_HEADS // DN_QK_HEADS
    assert rep == 2
    neg_a = -jnp.exp(alog_ref[...])
    dtb = dtb_ref[...]
    ng = ng_ref[...]

    def chunk(c, carry):
        r0 = pl.multiple_of(c * CHUNK, CHUNK)
        rows = pl.ds(r0, CHUNK)
        ba = zb_ref[rows, DN_VAL:DN_VAL + LANES]
        beta_all = _sigmoid(ba)
        g_all = neg_a * _softplus(ba + dtb)
        gc = _cumsum_rows(lower_ones, g_all)
        gct = gc.T
        eg_all = jnp.exp(gc)
        glast = gc[CHUNK - 1:CHUNK, :]
        eend_all = jnp.exp(glast - gc)
        elast_all = jnp.exp(glast)
        qs, ks = [], []
        for hq in range(DN_QK_HEADS):
            q = qkv_ref[rows, hq * DN_DK:(hq + 1) * DN_DK]
            k = qkv_ref[rows, DN_Q + hq * DN_DK:DN_Q + (hq + 1) * DN_DK]
            qs.append(q * lax.rsqrt(jnp.sum(q * q, axis=-1, keepdims=True) + 1e-6) * DN_DK ** -0.5)
            ks.append(k * lax.rsqrt(jnp.sum(k * k, axis=-1, keepdims=True) + 1e-6))
        k8 = jnp.stack(ks)
        qkk = _bmm(jnp.stack([jnp.concatenate([q, k], axis=0) for q, k in zip(qs, ks)]),
                   jnp.concatenate([k8, k8], axis=1), _NT)
        qk_p, kk_p = qkk[:, :CHUNK], qkk[:, CHUNK:]
        a_l, att_l = [], []
        for hq in range(DN_QK_HEADS):
            h0 = hq * rep
            beta_p = jnp.take_along_axis(beta_all, blk + h0, axis=1)
            g_col = jnp.take_along_axis(gc, blk + (DN_V_HEADS + h0), axis=1)
            g_row = jnp.concatenate([gct[DN_V_HEADS + h0 + e:DN_V_HEADS + h0 + e + 1, :] for e in range(rep)], axis=1)
            decay = jnp.where(incl_p, jnp.exp(g_col - g_row), 0.0)
            a_l.append(jnp.where(strict_p, kk_p[hq] * decay * beta_p, 0.0))
            att_l.append(qk_p[hq] * decay)
        t_inv = _unit_lower_inverse_pairs(jnp.stack(a_l), rowp, colp, blk)
        t_l, rhs_l, qe_l, kend_l, elast_l = [], [], [], [], []
        for h in range(DN_V_HEADS):
            hq, e = divmod(h, rep)
            gl = DN_V_HEADS + h
            beta = beta_all[:, h:h + 1]
            eg = eg_all[:, gl:gl + 1]
            v = qkv_ref[rows, 2 * DN_Q + h * DN_DV:2 * DN_Q + (h + 1) * DN_DV]
            t_l.append(t_inv[hq][:, e * CHUNK:(e + 1) * CHUNK])
            rhs_l.append(jnp.concatenate([v * beta, ks[hq] * (beta * eg)], axis=-1))
            qe_l.append(qs[hq] * eg)
            kend_l.append(ks[hq] * eend_all[:, gl:gl + 1])
            elast_l.append(elast_all[:, gl:gl + 1])
        sol = _bmm(jnp.stack(t_l), jnp.stack(rhs_l))
        u, wk = sol[:, :, :DN_DV], sol[:, :, DN_DV:]
        state = s_ref[...]
        ws = _bmm(jnp.concatenate([wk, jnp.stack(qe_l)], axis=1), state)
        v_new = u - ws[:, :CHUNK]
        zero = jnp.zeros((CHUNK, DN_DV), F32)
        vn_bd = jnp.stack([jnp.concatenate([jnp.concatenate([v_new[hq * rep], zero], axis=1),
                                            jnp.concatenate([zero, v_new[hq * rep + 1]], axis=1)], axis=0)
                           for hq in range(DN_QK_HEADS)])
        o_p = _bmm(jnp.stack(att_l), vn_bd)
        s_ref[...] = state * jnp.stack(elast_l) + _bmm(jnp.stack(kend_l), v_new, _TN)
        for h in range(DN_V_HEADS):
            hq, e = divmod(h, rep)
            vsl = slice(h * DN_DV, (h + 1) * DN_DV)
            o = ws[h, CHUNK:] + o_p[hq][:, e * DN_DV:(e + 1) * DN_DV]
            z = zb_ref[rows, vsl]
            o_ref[rows, vsl] = (_rms(o, ng) * _silu(z)).astype(o_ref.dtype)
        return carry

    lax.fori_loop(0, tb // CHUNK, chunk, 0, unroll=2)


def _dn_scan(qkv, zb, a_log_row, dt_bias_row, norm_g, *, batch, seq, tb):
    t = qkv.shape[0]
    nt = seq // tb
    return pl.pallas_call(
        functools.partial(_dn_scan_kernel, tb=tb),
        grid=(batch, nt),
        in_specs=[pl.BlockSpec((tb, DN_CONV_DIM), lambda b, i: (b * nt + i, 0)),
                  pl.BlockSpec((tb, DN_VAL + LANES), lambda b, i: (b * nt + i, 0)),
                  _resident((1, LANES)), _resident((1, LANES)), _resident((1, DN_DV))],
        out_specs=pl.BlockSpec((tb, DN_VAL), lambda b, i: (b * nt + i, 0)),
        out_shape=jax.ShapeDtypeStruct((t, DN_VAL), BF16),
        scratch_shapes=[pltpu.VMEM((DN_V_HEADS, DN_DK, DN_DV), F32)],
        compiler_params=_cparams("parallel", "arbitrary"),
        name="dn_scan",
    )(qkv, zb, a_log_row, dt_bias_row, norm_g)


def _ssd_scan_kernel(xbc_ref, zd_ref, alog_ref, dtb_ref, dskip_ref, ng_ref, o_ref, s_ref, *, tb):
    @pl.when(pl.program_id(1) == 0)
    def _():
        s_ref[...] = jnp.zeros_like(s_ref)

    row, col = _tri_masks(CHUNK)
    lower_ones = (col <= row).astype(F32)
    neg_a = -jnp.exp(alog_ref[...])
    dtb = dtb_ref[...]
    gsz = SSD_HPG * SSD_HEADDIM
    lane = lax.broadcasted_iota(jnp.int32, (CHUNK, LANES), 1)
    pair = lane // SSD_HEADDIM
    rowg = lax.broadcasted_iota(jnp.int32, (CHUNK, gsz), 0)
    laneg = lax.broadcasted_iota(jnp.int32, (CHUNK, gsz), 1)
    incl_g = (laneg & (SSD_HEADDIM - 1)) <= rowg
    head_g = laneg // SSD_HEADDIM

    def spread(cols, g):
        halves = [jnp.take_along_axis(cols, pair + (g * SSD_HPG + 2 * p), axis=1) for p in range(2)]
        return jnp.concatenate(halves, axis=1)

    def chunk(c, carry):
        r0 = pl.multiple_of(c * CHUNK, CHUNK)
        rows = pl.ds(r0, CHUNK)
        dt_all = _softplus(zd_ref[rows, SSD_INNER:SSD_INNER + LANES] + dtb)
        ac = _cumsum_rows(lower_ones, dt_all * neg_a)
        act = ac.T
        bm = jnp.stack([xbc_ref[rows, SSD_INNER + g * SSD_STATE:SSD_INNER + (g + 1) * SSD_STATE]
                        for g in range(SSD_GROUPS)])
        cm = jnp.stack([xbc_ref[rows, SSD_INNER + SSD_BC + g * SSD_STATE:SSD_INNER + SSD_BC + (g + 1) * SSD_STATE]
                        for g in range(SSD_GROUPS)])
        cb = _bmm(cm, jnp.concatenate([bm] * SSD_HPG, axis=1), _NT)
        state = s_ref[...]
        y_state = _bmm(cm, state)
        w_l, blk_l, xe_l, el_l, ea_l = [], [], [], [], []
        for g in range(SSD_GROUPS):
            xg = xbc_ref[rows, g * gsz:(g + 1) * gsz]
            ac_col = spread(ac, g)
            ac_row = jnp.concatenate([act[g * SSD_HPG + e:g * SSD_HPG + e + 1, :] for e in range(SSD_HPG)], axis=1)
            a_last = ac_col[CHUNK - 1:CHUNK, :]
            xdt = xg * spread(dt_all, g)
            w_l.append(cb[g] * jnp.where(incl_g, jnp.exp(ac_col - ac_row), 0.0))
            blk_l.append(jnp.concatenate([jnp.where(head_g == e, xdt, 0.0) for e in range(SSD_HPG)], axis=0))
            xe_l.append(xdt * jnp.exp(a_last - ac_col))
            el_l.append(jnp.exp(a_last))
            ea_l.append(jnp.exp(ac_col))
        y_intra = _bmm(jnp.stack(w_l), jnp.stack(blk_l))
        s_ref[...] = state * jnp.stack(el_l) + _bmm(bm, jnp.stack(xe_l), _TN)
        for g in range(SSD_GROUPS):
            gsl = slice(g * gsz, (g + 1) * gsz)
            y = y_intra[g] + y_state[g] * ea_l[g]
            y = y + xbc_ref[rows, gsl] * dskip_ref[:, gsl]
            yz = y * _silu(zd_ref[rows, gsl])
            o_ref[rows, gsl] = _rms(yz, ng_ref[:, gsl]).astype(o_ref.dtype)
        return carry

    lax.fori_loop(0, tb // CHUNK, chunk, 0, unroll=4)


def _ssd_scan(xbc, zd, a_log_row, dt_bias_row, d_skip_row, norm_g, *, batch, seq, tb):
    t = xbc.shape[0]
    nt = seq // tb
    return pl.pallas_call(
        functools.partial(_ssd_scan_kernel, tb=tb),
        grid=(batch, nt),
        in_specs=[pl.BlockSpec((tb, SSD_CONV_DIM), lambda b, i: (b * nt + i, 0)),
                  pl.BlockSpec((tb, SSD_INNER + LANES), lambda b, i: (b * nt + i, 0)),
                  _resident((1, LANES)), _resident((1, LANES)),
                  _resident((1, SSD_INNER)), _resident((1, SSD_INNER))],
        out_specs=pl.BlockSpec((tb, SSD_INNER), lambda b, i: (b * nt + i, 0)),
        out_shape=jax.ShapeDtypeStruct((t, SSD_INNER), BF16),
        scratch_shapes=[pltpu.VMEM((SSD_GROUPS, SSD_STATE, SSD_HPG * SSD_HEADDIM), F32)],
        compiler_params=_cparams("parallel", "arbitrary"),
        name="ssd_scan",
    )(xbc, zd, a_log_row, dt_bias_row, d_skip_row, norm_g)


def _rw_proj_kernel(x_ref, xh_ref, gpre_ref, mu_ref, wrkv_ref, w0_ref, w1_ref, w2_ref, a0_ref, a1_ref, a2_ref,
                    g1_ref, g2_ref, kk_ref, ka_ref,
                    r_out, lw_out, k_out, v_out, kk_out, a_out, gate_out, *, tm, tiles_per_seq):
    i = pl.program_id(0)
    g = gpre_ref[...]
    first = (i % tiles_per_seq) == 0
    x = _rms(x_ref[...], g)
    xh = _rms(jnp.where(first, 0.0, xh_ref[...]), g)
    prev = jnp.concatenate([xh, x], axis=0)[HALO - 1:HALO - 1 + tm]
    xx = prev - x
    mu = mu_ref[...]
    xr, xw, xk, xv, xa, xg = [(x + xx * mu[j:j + 1]) for j in range(6)]
    r = _mm(xr, wrkv_ref[0])
    k = _mm(xk, wrkv_ref[1])
    v = _mm(xv, wrkv_ref[2])
    w = -_softplus(-(w0_ref[...] + _mm(jnp.tanh(_mm(xw, w1_ref[...])), w2_ref[...]))) - 0.5
    a = _sigmoid(a0_ref[...] + _mm(_mm(xa, a1_ref[...]), a2_ref[...]))
    gate = _mm(_sigmoid(_mm(xg, g1_ref[...])), g2_ref[...])
    r_out[...] = r.astype(r_out.dtype)
    lw_out[...] = -jnp.exp(w)
    k_out[...] = (k * (1.0 + (a - 1.0) * ka_ref[...])).astype(k_out.dtype)
    v_out[...] = v.astype(v_out.dtype)
    kk_out[...] = (k * kk_ref[...]).astype(kk_out.dtype)
    a_out[...] = a.astype(a_out.dtype)
    gate_out[...] = gate.astype(gate_out.dtype)


def _rw_proj(h, g_pre, mu, w_rkv, w0, w1, w2, a0, a1, a2, g1, g2, k_k, k_a, *, seq, tm):
    t, d = h.shape
    hb = tm // HALO
    row = lambda: _resident((1, d))
    tok = pl.BlockSpec((tm, d), lambda i: (i, 0))
    outs = [jax.ShapeDtypeStruct((t, d), F32 if j == 1 else BF16) for j in range(7)]
    return pl.pallas_call(
        functools.partial(_rw_proj_kernel, tm=tm, tiles_per_seq=seq // tm),
        grid=(t // tm,),
        in_specs=[tok, pl.BlockSpec((HALO, d), lambda i: (jnp.maximum(i * hb - 1, 0), 0)),
                  row(), _resident(mu.shape), _resident(w_rkv.shape), row(), _resident(w1.shape),
                  _resident(w2.shape), row(), _resident(a1.shape), _resident(a2.shape),
                  _resident(g1.shape), _resident(g2.shape), row(), row()],
        out_specs=[tok] * 7,
        out_shape=outs,
        compiler_params=_cparams("parallel"),
        name="rw_proj",
    )(h, h, g_pre, mu, w_rkv, w0, w1, w2, a0, a1, a2, g1, g2, k_k, k_a)


def _rw_scan_kernel(r_ref, lw_ref, k_ref, v_ref, kk_ref, a_ref, gate_ref, rk_ref, lng_ref, lnb_ref,
                    o_ref, s_ref, *, tb):
    @pl.when(pl.program_id(1) == 0)
    def _():
        s_ref[...] = jnp.zeros_like(s_ref)

    row, col = _tri_masks(CHUNK)
    lower_ones = (col <= row).astype(F32)
    rowp, colp, blk = _pair_masks(CHUNK)
    assert RW_HEAD == CHUNK and 2 * RW_HEAD == LANES
    incl_p = colp <= rowp
    strict_p = colp < rowp
    first = blk == 0
    keep1 = blk[:1].astype(F32)
    keep0 = 1.0 - keep1
    npairs = RW_HEADS // 2

    def pairs(t):
        return jnp.stack([t[:, p * LANES:(p + 1) * LANES] for p in range(npairs)])

    def flat(t):
        return jnp.concatenate([t[p] for p in range(npairs)], axis=-1)

    def head_sum(t):
        s0 = jnp.sum(jnp.where(first, t, 0.0), axis=-1, keepdims=True)
        s1 = jnp.sum(jnp.where(first, 0.0, t), axis=-1, keepdims=True)
        return jnp.where(first, s0, s1)

    def to_pairs(rows0, rows1):
        bk0, bk1 = rows0[:, :, :LANES], rows1[:, :, :LANES]
        m_b = jnp.where(first, bk0, pltpu.roll(bk1, RW_HEAD, axis=2))
        m_k = jnp.where(first, pltpu.roll(bk0, RW_HEAD, axis=2), bk1)
        m_s = jnp.where(first, rows0[:, :, LANES:], rows1[:, :, LANES:])
        return m_b, m_k, m_s

    def chunk(c, carry):
        r0 = pl.multiple_of(c * CHUNK, CHUNK)
        rows = pl.ds(r0, CHUNK)
        lw = lw_ref[rows, :]
        wc = _cumsum_rows(lower_ones, lw)
        e_inv = jnp.exp(-wc)
        r = r_ref[rows, :].astype(F32)
        k = k_ref[rows, :].astype(F32)
        vp = pairs(v_ref[rows, :].astype(F32))
        kk = pairs(kk_ref[rows, :].astype(F32))
        kk = kk * lax.rsqrt(head_sum(kk * kk) + 1e-6)
        rt = pairs(r * jnp.exp(wc))
        kt = pairs(k * e_inv)
        at = -kk * pairs(jnp.exp(wc - lw))
        bt = kk * pairs(a_ref[rows, :].astype(F32) * e_inv)
        state = s_ref[...]
        ar = jnp.concatenate([at, rt], axis=1)
        p = _bmm(jnp.concatenate([ar * keep0, ar * keep1], axis=1),
                 jnp.concatenate([bt, kt, state, state], axis=1), _NT)
        a_ab, a_ak, a_s = to_pairs(p[:, :CHUNK], p[:, 2 * CHUNK:3 * CHUNK])
        r_b, r_k, r_s = to_pairs(p[:, CHUNK:2 * CHUNK], p[:, 3 * CHUNK:])
        t_inv = _unit_lower_inverse_pairs(jnp.where(strict_p, -a_ab, 0.0), rowp, colp, blk)
        v_bd = _block_diag(vp, blk)
        u = _bmm(t_inv, _block_diag(a_s + _bmm(jnp.where(strict_p, a_ak, 0.0), v_bd), blk))
        m = jnp.concatenate([jnp.where(incl_p, r_b, 0.0), jnp.where(incl_p, r_k, 0.0)], axis=2)
        y = r_s + _bmm(m, jnp.concatenate([_block_diag(u, blk), v_bd], axis=1))
        prod = _bmm(jnp.concatenate([u, vp], axis=1), jnp.concatenate([bt, kt], axis=1), _TN)
        upd = jnp.where(first, prod[:, :RW_HEAD], prod[:, RW_HEAD:])
        s_ref[...] = (state + upd) * pairs(jnp.exp(wc[CHUNK - 1:CHUNK, :]))
        yc = y - head_sum(y) * (1.0 / RW_HEAD)
        yn = yc * lax.rsqrt(head_sum(yc * yc) * (1.0 / RW_HEAD) + RW_GN_EPS)
        bonus = head_sum(pairs(r * k * rk_ref[...])) * vp
        out = (flat(yn) * lng_ref[...] + lnb_ref[...] + flat(bonus)) * gate_ref[rows, :].astype(F32)
        o_ref[rows, :] = out.astype(o_ref.dtype)
        return carry

    lax.fori_loop(0, tb // CHUNK, chunk, 0, unroll=2)


def _rw_scan(r, lw, k, v, kk, a, gate, r_k, ln_g, ln_b, *, batch, seq, tb):
    t, d = r.shape
    nt = seq // tb
    tok = pl.BlockSpec((tb, d), lambda bi, i: (bi * nt + i, 0))
    return pl.pallas_call(
        functools.partial(_rw_scan_kernel, tb=tb),
        grid=(batch, nt),
        in_specs=[tok] * 7 + [_resident((1, d))] * 3,
        out_specs=tok,
        out_shape=jax.ShapeDtypeStruct((t, d), BF16),
        scratch_shapes=[pltpu.VMEM((RW_HEADS // 2, RW_HEAD, 2 * RW_HEAD), F32)],
        compiler_params=_cparams("parallel", "arbitrary"),
        name="rw_scan",
    )(r, lw, k, v, kk, a, gate, r_k, ln_g, ln_b)


def _row(v):
    return v.reshape(1, -1).astype(F32)


def _lane_row(v, offset):
    return jnp.zeros((1, LANES), F32).at[0, offset:offset + v.shape[0]].set(v.astype(F32))


def _pad_cols(w, n):
    return jnp.pad(w, ((0, 0), (0, n - w.shape[1])))


def _tile(seq, want):
    return min(seq, want)


def _deltanet_layer(h, g_pre, g_post, w_in, conv_w, a_log, dt_bias, norm_g, w_out, *, batch, seq):
    w_conv = w_in[:, :DN_CONV_DIM].astype(BF16)
    w_rest = _pad_cols(w_in[:, DN_CONV_DIM:], DN_VAL + LANES).astype(BF16)
    qkv = _norm_proj_conv(h, g_pre, w_conv, conv_w, jnp.zeros((1, DN_CONV_DIM), F32),
                          seq=seq, tm=_tile(seq, 512))
    zb = _norm_proj(h, g_pre, w_rest, tm=_tile(seq, 512))
    o = _dn_scan(qkv, zb, _lane_row(a_log, DN_V_HEADS), _lane_row(dt_bias, DN_V_HEADS), _row(norm_g),
                 batch=batch, seq=seq, tb=_tile(seq, 512))
    return _proj_post(o, h, w_out.astype(BF16), g_post, tm=_tile(seq, 512))


def _ssd_layer(h, g_pre, g_post, w_in, conv_w, conv_b, a_log, dt_bias, d_skip, norm_g, w_out, *, batch, seq):
    w_conv = w_in[:, SSD_INNER:SSD_INNER + SSD_CONV_DIM].astype(BF16)
    w_rest = _pad_cols(jnp.concatenate([w_in[:, :SSD_INNER], w_in[:, SSD_INNER + SSD_CONV_DIM:]], axis=1),
                       SSD_INNER + LANES).astype(BF16)
    xbc = _norm_proj_conv(h, g_pre, w_conv, conv_w, _row(conv_b), seq=seq, tm=_tile(seq, 512))
    zd = _norm_proj(h, g_pre, w_rest, tm=_tile(seq, 512))
    y = _ssd_scan(xbc, zd, _lane_row(a_log, 0), _lane_row(dt_bias, 0),
                  _row(jnp.repeat(d_skip, SSD_HEADDIM)), _row(norm_g), batch=batch, seq=seq, tb=_tile(seq, 512))
    return _proj_post(y, h, w_out.astype(BF16), g_post, tm=_tile(seq, 512))


def _rwkv_layer(h, g_pre, g_post, mu, w_rkv, w0, w1, w2, a0, a1, a2, g1, g2, k_k, k_a, r_k, ln_g, ln_b, w_out,
                *, batch, seq):
    bf = lambda w: w.astype(BF16)
    outs = _rw_proj(h, g_pre, mu.astype(F32), bf(w_rkv), _row(w0), bf(w1), bf(w2), _row(a0), bf(a1), bf(a2),
                    bf(g1), bf(g2), _row(k_k), _row(k_a), seq=seq, tm=_tile(seq, 256))
    y = _rw_scan(*outs, _row(r_k), _row(ln_g), _row(ln_b), batch=batch, seq=seq, tb=_tile(seq, 512))
    return _proj_post(y, h, bf(w_out), g_post, tm=_tile(seq, 512))


def kernel(x, mem, sandwich_g, ffn_w_in, ffn_w_out, mem_norm_g, xa_w_q, xa_w_kv, xa_w_o, dn_w_in, dn_conv_w, dn_a_log, dn_dt_bias, dn_norm_g, dn_w_out, ssd_w_in, ssd_conv_w, ssd_conv_b, ssd_a_log, ssd_dt_bias, ssd_d, ssd_norm_g, ssd_w_out, rw_mu, rw_w_rkv, rw_w0, rw_w1, rw_w2, rw_a0, rw_a1, rw_a2, rw_g1, rw_g2, rw_k_k, rw_k_a, rw_r_k, rw_ln_g, rw_ln_b, rw_w_out):
    batch, seq, d = x.shape
    n_mem = mem.shape[1]
    h = x.reshape(batch * seq, d)
    mem2 = mem.reshape(batch * n_mem, d)
    tm_ffn = _tile(seq, 512)

    def ffn(h, l, which, g_pre, g_post):
        w_in = ffn_w_in[l, which].astype(BF16)
        return _ffn(h, _row(g_pre), _row(g_post), w_in[:, :D_FF], w_in[:, D_FF:],
                    ffn_w_out[l, which].astype(BF16), tm=tm_ffn)

    for l in range(DEPTH):
        g = sandwich_g[l]
        kind = l % N_MIXERS
        j = l // N_MIXERS
        h = ffn(h, l, 0, g[0, 0], g[0, 1])
        gp, gq = _row(g[1, 0]), _row(g[1, 1])
        if kind == 0:
            h = _deltanet_layer(h, gp, gq, dn_w_in[j], dn_conv_w[j], dn_a_log[j], dn_dt_bias[j], dn_norm_g[j],
                                dn_w_out[j], batch=batch, seq=seq)
        elif kind == 1:
            h = _ssd_layer(h, gp, gq, ssd_w_in[j], ssd_conv_w[j], ssd_conv_b[j], ssd_a_log[j], ssd_dt_bias[j],
                           ssd_d[j], ssd_norm_g[j], ssd_w_out[j], batch=batch, seq=seq)
        else:
            h = _rwkv_layer(h, gp, gq, rw_mu[j], rw_w_rkv[j], rw_w0[j], rw_w1[j], rw_w2[j], rw_a0[j], rw_a1[j],
                            rw_a2[j], rw_g1[j], rw_g2[j], rw_k_k[j], rw_k_a[j], rw_r_k[j].reshape(-1),
                            rw_ln_g[j], rw_ln_b[j], rw_w_out[j], batch=batch, seq=seq)
        kv = _norm_proj(mem2, _row(mem_norm_g[l]), xa_w_kv[l].astype(BF16), tm=n_mem, out_dtype=BF16)
        h = _xattn(h, kv, _row(g[2, 0]), _row(g[2, 1]), xa_w_q[l].astype(BF16), xa_w_o[l].astype(BF16),
                   seq=seq, tm=_tile(seq, 512))
        h = ffn(h, l, 1, g[3, 0], g[3, 1])
    return h.reshape(batch, seq, d)
```

```python
import functools
import math

import jax
import jax.numpy as jnp
from jax import lax
from jax.experimental import pallas as pl
from jax.experimental.pallas import tpu as pltpu

F32 = jnp.float32
BF16 = jnp.bfloat16

D_MODEL = 1024
DEPTH = 4
N_MIXERS = 3
CHUNK = 64
CONV_W = 4
RMS_EPS = 1e-6
MACARON_W = 0.5
D_FF = 2816

DN_QK_HEADS = 8
DN_V_HEADS = 16
DN_DK = 128
DN_DV = 128
DN_Q = DN_QK_HEADS * DN_DK
DN_VAL = DN_V_HEADS * DN_DV
DN_CONV_DIM = 2 * DN_Q + DN_VAL

SSD_INNER = 2 * D_MODEL
SSD_HEADDIM = 64
SSD_HEADS = SSD_INNER // SSD_HEADDIM
SSD_GROUPS = 8
SSD_HPG = SSD_HEADS // SSD_GROUPS
SSD_STATE = 128
SSD_BC = SSD_GROUPS * SSD_STATE
SSD_CONV_DIM = SSD_INNER + 2 * SSD_BC

RW_HEAD = 64
RW_HEADS = D_MODEL // RW_HEAD
RW_GN_EPS = 64e-5

XA_HEADS = 4
XA_DH = D_MODEL // XA_HEADS

LANES = 128
SUBLANES = 8
HALO = 16
VMEM_LIMIT = 56 * 1024 * 1024
HI = lax.Precision.HIGHEST


def _cparams(*sem):
    return pltpu.CompilerParams(dimension_semantics=sem, vmem_limit_bytes=VMEM_LIMIT)


def _resident(shape):
    nd = len(shape)
    return pl.BlockSpec(shape, lambda *_: (0,) * nd, pipeline_mode=pl.Buffered(1))


def _rms(x, g, eps=RMS_EPS):
    return x * lax.rsqrt(jnp.mean(x * x, axis=-1, keepdims=True) + eps) * g


def _sigmoid(x):
    return 0.5 * jnp.tanh(0.5 * x) + 0.5


def _half_silu(h):
    return h + h * jnp.tanh(h)


def _silu(x):
    return _half_silu(0.5 * x)


def _softplus(x):
    return jnp.maximum(x, 0.0) + jnp.log1p(jnp.exp(-jnp.abs(x)))


def _mm(a, b):
    return jnp.dot(a.astype(BF16), b.astype(BF16), preferred_element_type=F32)


def _mm_nt(a, b):
    return lax.dot_general(a.astype(BF16), b.astype(BF16), (((1,), (1,)), ((), ())),
                           preferred_element_type=F32)


def _mm_tn(a, b):
    return lax.dot_general(a.astype(BF16), b.astype(BF16), (((0,), (0,)), ((), ())),
                           preferred_element_type=F32)


def _mm_hi(a, b):
    return jnp.dot(a, b, preferred_element_type=F32, precision=HI)


def _tri_masks(n):
    row = lax.broadcasted_iota(jnp.int32, (n, n), 0)
    col = lax.broadcasted_iota(jnp.int32, (n, n), 1)
    return row, col


_NN = ((2,), (1,))
_NT = ((2,), (2,))
_TN = ((1,), (1,))


def _bmm(a, b, dims=_NN):
    return lax.dot_general(a.astype(BF16), b.astype(BF16), (dims, ((0,), (0,))), preferred_element_type=F32)


def _unit_lower_inverse(a, row, col):
    n = a.shape[-1]
    eye = (row == col).astype(F32)
    x = eye - jnp.where((row >> 1) == (col >> 1), a, 0.0)
    b = 2
    while b < n:
        sh = b.bit_length() - 1
        m = ((row >> (sh + 1)) == (col >> (sh + 1))) & (((row >> sh) & 1) == 1) & (((col >> sh) & 1) == 0)
        lm = jnp.where(m, a, 0.0)
        x = x - _bmm(_bmm(x, lm), x)
        b *= 2
    return x


def _pair_masks(n):
    row = lax.broadcasted_iota(jnp.int32, (n, 2 * n), 0)
    lane = lax.broadcasted_iota(jnp.int32, (n, 2 * n), 1)
    return row, lane & (n - 1), lane // n


def _block_diag(p, blk):
    pb = p.astype(BF16)
    keep1 = blk.astype(BF16)
    return jnp.concatenate([pb * (1 - keep1), pb * keep1], axis=1)


def _unit_lower_inverse_pairs(a, row, col, blk):
    n = a.shape[1]
    eye = (row == col).astype(F32)
    x = eye - jnp.where((row >> 1) == (col >> 1), a, 0.0)
    b = 2
    while b < n:
        sh = b.bit_length() - 1
        m = ((row >> (sh + 1)) == (col >> (sh + 1))) & (((row >> sh) & 1) == 1) & (((col >> sh) & 1) == 0)
        lm = jnp.where(m, a, 0.0)
        x = x - _bmm(_bmm(x, _block_diag(lm, blk)), _block_diag(x, blk))
        b *= 2
    return x


def _cumsum_rows(lower_ones, v):
    return _mm_hi(lower_ones, v)


def _norm_proj_kernel(x_ref, g_ref, w_ref, o_ref):
    xn = _rms(x_ref[...], g_ref[...]).astype(BF16)
    o_ref[...] = jnp.dot(xn, w_ref[...], preferred_element_type=F32).astype(o_ref.dtype)


def _norm_proj(x, g, w, *, tm, out_dtype=F32):
    t, d = x.shape
    n = w.shape[1]
    return pl.pallas_call(
        _norm_proj_kernel,
        grid=(t // tm,),
        in_specs=[pl.BlockSpec((tm, d), lambda i: (i, 0)), _resident((1, d)), _resident((d, n))],
        out_specs=pl.BlockSpec((tm, n), lambda i: (i, 0)),
        out_shape=jax.ShapeDtypeStruct((t, n), out_dtype),
        compiler_params=_cparams("parallel"),
        name="norm_proj",
    )(x, g, w)


def _norm_proj_conv_kernel(x_ref, xh_ref, g_ref, w_ref, cw_ref, cb_ref, o_ref, p_ref, *, tm, tn, tiles_per_seq):
    i = pl.program_id(0)
    g = g_ref[...]
    first = (i % tiles_per_seq) == 0
    xh = jnp.where(first, 0.0, xh_ref[...])
    xn = jnp.concatenate([_rms(xh, g), _rms(x_ref[...], g)], axis=0).astype(BF16)
    n = w_ref.shape[1]
    for c in range(n // tn):
        sl = slice(c * tn, (c + 1) * tn)
        p_ref[c % 2] = jnp.dot(xn, w_ref[:, sl], preferred_element_type=F32)
        cw = 0.5 * cw_ref[:, sl]
        hy = 0.5 * cb_ref[:, sl]
        for j in range(CONV_W):
            hy = hy + cw[j:j + 1] * p_ref[c % 2, pl.ds(HALO - (CONV_W - 1) + j, tm), :]
        o_ref[:, sl] = _half_silu(hy)


def _norm_proj_conv(x, g, w, conv_w, conv_b, *, seq, tm, tn=512):
    t, d = x.shape
    n = w.shape[1]
    hb = tm // HALO
    kern = functools.partial(_norm_proj_conv_kernel, tm=tm, tn=tn, tiles_per_seq=seq // tm)
    return pl.pallas_call(
        kern,
        grid=(t // tm,),
        in_specs=[pl.BlockSpec((tm, d), lambda i: (i, 0)),
                  pl.BlockSpec((HALO, d), lambda i: (jnp.maximum(i * hb - 1, 0), 0)),
                  _resident((1, d)), _resident((d, n)), _resident((CONV_W, n)), _resident((1, n))],
        out_specs=pl.BlockSpec((tm, n), lambda i: (i, 0)),
        out_shape=jax.ShapeDtypeStruct((t, n), F32),
        scratch_shapes=[pltpu.VMEM((2, HALO + tm, tn), F32)],
        compiler_params=_cparams("parallel"),
        name="norm_proj_conv",
    )(x, x, g, w, conv_w, conv_b)


def _proj_post_kernel(a_ref, h_ref, w_ref, g_ref, o_ref, *, weight):
    u = jnp.dot(a_ref[...], w_ref[...], preferred_element_type=F32)
    o_ref[...] = h_ref[...] + weight * _rms(u, g_ref[...])


def _proj_post(a, h, w, g_post, *, tm, weight=1.0):
    t, k = a.shape
    d = w.shape[1]
    return pl.pallas_call(
        functools.partial(_proj_post_kernel, weight=weight),
        grid=(t // tm,),
        in_specs=[pl.BlockSpec((tm, k), lambda i: (i, 0)), pl.BlockSpec((tm, d), lambda i: (i, 0)),
                  _resident((k, d)), _resident((1, d))],
        out_specs=pl.BlockSpec((tm, d), lambda i: (i, 0)),
        out_shape=jax.ShapeDtypeStruct((t, d), F32),
        compiler_params=_cparams("parallel"),
        name="proj_post",
    )(a, h, w, g_post)


def _ffn_kernel(h_ref, gpre_ref, gpost_ref, wg_ref, wu_ref, wo_ref, o_ref, acc_ref, *, tf):
    h = h_ref[...]
    xn = _rms(h, gpre_ref[...]).astype(BF16)
    nf = wg_ref.shape[1] // tf
    for c in range(nf):
        sl = slice(c * tf, (c + 1) * tf)
        gate = jnp.dot(xn, wg_ref[:, sl], preferred_element_type=F32)
        up = jnp.dot(xn, wu_ref[:, sl], preferred_element_type=F32)
        act = (_silu(gate) * up).astype(BF16)
        part = jnp.dot(act, wo_ref[sl, :], preferred_element_type=F32)
        if c == 0:
            acc_ref[...] = part
        else:
            acc_ref[...] += part
    o_ref[...] = h + MACARON_W * _rms(acc_ref[...], gpost_ref[...])


def _ffn(h, g_pre, g_post, w_in, w_out, *, tm, tf=256):
    t, d = h.shape
    f = w_out.shape[0]
    half = lambda j: pl.BlockSpec((d, f), lambda i: (0, j), pipeline_mode=pl.Buffered(1))
    return pl.pallas_call(
        functools.partial(_ffn_kernel, tf=tf),
        grid=(t // tm,),
        in_specs=[pl.BlockSpec((tm, d), lambda i: (i, 0)), _resident((1, d)), _resident((1, d)),
                  half(0), half(1), _resident((f, d))],
        out_specs=pl.BlockSpec((tm, d), lambda i: (i, 0)),
        out_shape=jax.ShapeDtypeStruct((t, d), F32),
        scratch_shapes=[pltpu.VMEM((tm, d), F32)],
        compiler_params=_cparams("parallel"),
        name="ffn",
    )(h, g_pre, g_post, w_in, w_in, w_out)


def _xattn_kernel(h_ref, kv_ref, gpre_ref, gpost_ref, wq_ref, wo_ref, o_ref):
    h = h_ref[...]
    d = h.shape[1]
    xn = _rms(h, gpre_ref[...]).astype(BF16)
    q = jnp.dot(xn, wq_ref[...], preferred_element_type=F32) * XA_DH ** -0.5
    outs = []
    for hh in range(XA_HEADS):
        sl = slice(hh * XA_DH, (hh + 1) * XA_DH)
        k = kv_ref[:, sl]
        v = kv_ref[:, d + hh * XA_DH:d + (hh + 1) * XA_DH]
        s = _mm_nt(q[:, sl], k)
        s = s - jnp.max(s, axis=-1, keepdims=True)
        e = jnp.exp(s)
        p = e / jnp.sum(e, axis=-1, keepdims=True)
        outs.append(_mm(p, v))
    o = jnp.concatenate(outs, axis=-1).astype(BF16)
    u = jnp.dot(o, wo_ref[...], preferred_element_type=F32)
    o_ref[...] = h + _rms(u, gpost_ref[...])


def _xattn(h, kv, g_pre, g_post, w_q, w_o, *, seq, tm):
    t, d = h.shape
    n_mem = kv.shape[0] // (t // seq)
    tiles_per_seq = seq // tm
    return pl.pallas_call(
        _xattn_kernel,
        grid=(t // tm,),
        in_specs=[pl.BlockSpec((tm, d), lambda i: (i, 0)),
                  pl.BlockSpec((n_mem, 2 * d), lambda i: (i // tiles_per_seq, 0)),
                  _resident((1, d)), _resident((1, d)), _resident((d, d)), _resident((d, d))],
        out_specs=pl.BlockSpec((tm, d), lambda i: (i, 0)),
        out_shape=jax.ShapeDtypeStruct((t, d), F32),
        compiler_params=_cparams("parallel"),
        name="xattn",
    )(h, kv, g_pre, g_post, w_q, w_o)


def _dn_scan_kernel(qkv_ref, zb_ref, alog_ref, dtb_ref, ng_ref, o_ref, s_ref, *, tb):
    @pl.when(pl.program_id(1) == 0)
    def _():
        s_ref[...] = jnp.zeros_like(s_ref)

    row, col = _tri_masks(CHUNK)
    lower_ones = (col <= row).astype(F32)
    rowp, colp, blk = _pair_masks(CHUNK)
    incl_p = colp <= rowp
    strict_p = colp < rowp
    rep = DN_V_HEADS // DN_QK_HEADS
    assert rep == 2
    neg_a = -jnp.exp(alog_ref[...])
    dtb = dtb_ref[...]
    ng = ng_ref[...]

    def chunk(c, carry):
        r0 = pl.multiple_of(c * CHUNK, CHUNK)
        rows = pl.ds(r0, CHUNK)
        ba = zb_ref[rows, DN_VAL:DN_VAL + LANES]
        beta_all = _sigmoid(ba)
        g_all = neg_a * _softplus(ba + dtb)
        gc = _cumsum_rows(lower_ones, g_all)
        gct = gc.T
        eg_all = jnp.exp(gc)
        glast = gc[CHUNK - 1:CHUNK, :]
        eend_all = jnp.exp(glast - gc)
        elast_all = jnp.exp(glast)
        qs, ks = [], []
        for hq in range(DN_QK_HEADS):
            q = qkv_ref[rows, hq * DN_DK:(hq + 1) * DN_DK]
            k = qkv_ref[rows, DN_Q + hq * DN_DK:DN_Q + (hq + 1) * DN_DK]
            qs.append(q * lax.rsqrt(jnp.sum(q * q, axis=-1, keepdims=True) + 1e-6) * DN_DK ** -0.5)
            ks.append(k * lax.rsqrt(jnp.sum(k * k, axis=-1, keepdims=True) + 1e-6))
        k8 = jnp.stack(ks)
        qkk = _bmm(jnp.stack([jnp.concatenate([q, k], axis=0) for q, k in zip(qs, ks)]),
                   jnp.concatenate([k8, k8], axis=1), _NT)
        qk_p, kk_p = qkk[:, :CHUNK], qkk[:, CHUNK:]
        a_l, att_l = [], []
        for hq in range(DN_QK_HEADS):
            h0 = hq * rep
            beta_p = jnp.take_along_axis(beta_all, blk + h0, axis=1)
            g_col = jnp.take_along_axis(gc, blk + (DN_V_HEADS + h0), axis=1)
            g_row = jnp.concatenate([gct[DN_V_HEADS + h0 + e:DN_V_HEADS + h0 + e + 1, :] for e in range(rep)], axis=1)
            decay = jnp.where(incl_p, jnp.exp(g_col - g_row), 0.0)
            a_l.append(jnp.where(strict_p, kk_p[hq] * decay * beta_p, 0.0))
            att_l.append(qk_p[hq] * decay)
        t_inv = _unit_lower_inverse_pairs(jnp.stack(a_l), rowp, colp, blk)
        t_l, rhs_l, qe_l, kend_l, elast_l = [], [], [], [], []
        for h in range(DN_V_HEADS):
            hq, e = divmod(h, rep)
            gl = DN_V_HEADS + h
            beta = beta_all[:, h:h + 1]
            eg = eg_all[:, gl:gl + 1]
            v = qkv_ref[rows, 2 * DN_Q + h * DN_DV:2 * DN_Q + (h + 1) * DN_DV]
            t_l.append(t_inv[hq][:, e * CHUNK:(e + 1) * CHUNK])
            rhs_l.append(jnp.concatenate([v * beta, ks[hq] * (beta * eg)], axis=-1))
            qe_l.append(qs[hq] * eg)
            kend_l.append(ks[hq] * eend_all[:, gl:gl + 1])
            elast_l.append(elast_all[:, gl:gl + 1])
        sol = _bmm(jnp.stack(t_l), jnp.stack(rhs_l))
        u, wk = sol[:, :, :DN_DV], sol[:, :, DN_DV:]
        state = s_ref[...]
        ws = _bmm(jnp.concatenate([wk, jnp.stack(qe_l)], axis=1), state)
        v_new = u - ws[:, :CHUNK]
        zero = jnp.zeros((CHUNK, DN_DV), F32)
        vn_bd = jnp.stack([jnp.concatenate([jnp.concatenate([v_new[hq * rep], zero], axis=1),
                                            jnp.concatenate([zero, v_new[hq * rep + 1]], axis=1)], axis=0)
                           for hq in range(DN_QK_HEADS)])
        o_p = _bmm(jnp.stack(att_l), vn_bd)
        s_ref[...] = state * jnp.stack(elast_l) + _bmm(jnp.stack(kend_l), v_new, _TN)
        for h in range(DN_V_HEADS):
            hq, e = divmod(h, rep)
            vsl = slice(h * DN_DV, (h + 1) * DN_DV)
            o = ws[h, CHUNK:] + o_p[hq][:, e * DN_DV:(e + 1) * DN_DV]
            z = zb_ref[rows, vsl]
            o_ref[rows, vsl] = (_rms(o, ng) * _silu(z)).astype(o_ref.dtype)
        return carry

    lax.fori_loop(0, tb // CHUNK, chunk, 0, unroll=2)


def _dn_scan(qkv, zb, a_log_row, dt_bias_row, norm_g, *, batch, seq, tb):
    t = qkv.shape[0]
    nt = seq // tb
    return pl.pallas_call(
        functools.partial(_dn_scan_kernel, tb=tb),
        grid=(batch, nt),
        in_specs=[pl.BlockSpec((tb, DN_CONV_DIM), lambda b, i: (b * nt + i, 0)),
                  pl.BlockSpec((tb, DN_VAL + LANES), lambda b, i: (b * nt + i, 0)),
                  _resident((1, LANES)), _resident((1, LANES)), _resident((1, DN_DV))],
        out_specs=pl.BlockSpec((tb, DN_VAL), lambda b, i: (b * nt + i, 0)),
        out_shape=jax.ShapeDtypeStruct((t, DN_VAL), BF16),
        scratch_shapes=[pltpu.VMEM((DN_V_HEADS, DN_DK, DN_DV), F32)],
        compiler_params=_cparams("parallel", "arbitrary"),
        name="dn_scan",
    )(qkv, zb, a_log_row, dt_bias_row, norm_g)


def _ssd_scan_kernel(xbc_ref, zd_ref, alog_ref, dtb_ref, dskip_ref, ng_ref, o_ref, s_ref, *, tb):
    @pl.when(pl.program_id(1) == 0)
    def _():
        s_ref[...] = jnp.zeros_like(s_ref)

    row, col = _tri_masks(CHUNK)
    lower_ones = (col <= row).astype(F32)
    neg_a = -jnp.exp(alog_ref[...])
    dtb = dtb_ref[...]
    gsz = SSD_HPG * SSD_HEADDIM
    lane = lax.broadcasted_iota(jnp.int32, (CHUNK, LANES), 1)
    pair = lane // SSD_HEADDIM
    rowg = lax.broadcasted_iota(jnp.int32, (CHUNK, gsz), 0)
    laneg = lax.broadcasted_iota(jnp.int32, (CHUNK, gsz), 1)
    incl_g = (laneg & (SSD_HEADDIM - 1)) <= rowg
    head_g = laneg // SSD_HEADDIM

    def spread(cols, g):
        halves = [jnp.take_along_axis(cols, pair + (g * SSD_HPG + 2 * p), axis=1) for p in range(2)]
        return jnp.concatenate(halves, axis=1)

    def chunk(c, carry):
        r0 = pl.multiple_of(c * CHUNK, CHUNK)
        rows = pl.ds(r0, CHUNK)
        dt_all = _softplus(zd_ref[rows, SSD_INNER:SSD_INNER + LANES] + dtb)
        ac = _cumsum_rows(lower_ones, dt_all * neg_a)
        act = ac.T
        bm = jnp.stack([xbc_ref[rows, SSD_INNER + g * SSD_STATE:SSD_INNER + (g + 1) * SSD_STATE]
                        for g in range(SSD_GROUPS)])
        cm = jnp.stack([xbc_ref[rows, SSD_INNER + SSD_BC + g * SSD_STATE:SSD_INNER + SSD_BC + (g + 1) * SSD_STATE]
                        for g in range(SSD_GROUPS)])
        cb = _bmm(cm, jnp.concatenate([bm] * SSD_HPG, axis=1), _NT)
        state = s_ref[...]
        y_state = _bmm(cm, state)
        w_l, blk_l, xe_l, el_l, ea_l = [], [], [], [], []
        for g in range(SSD_GROUPS):
            xg = xbc_ref[rows, g * gsz:(g + 1) * gsz]
            ac_col = spread(ac, g)
            ac_row = jnp.concatenate([act[g * SSD_HPG + e:g * SSD_HPG + e + 1, :] for e in range(SSD_HPG)], axis=1)
            a_last = ac_col[CHUNK - 1:CHUNK, :]
            xdt = xg * spread(dt_all, g)
            w_l.append(cb[g] * jnp.where(incl_g, jnp.exp(ac_col - ac_row), 0.0))
            blk_l.append(jnp.concatenate([jnp.where(head_g == e, xdt, 0.0) for e in range(SSD_HPG)], axis=0))
            xe_l.append(xdt * jnp.exp(a_last - ac_col))
            el_l.append(jnp.exp(a_last))
            ea_l.append(jnp.exp(ac_col))
        y_intra = _bmm(jnp.stack(w_l), jnp.stack(blk_l))
        s_ref[...] = state * jnp.stack(el_l) + _bmm(bm, jnp.stack(xe_l), _TN)
        for g in range(SSD_GROUPS):
            gsl = slice(g * gsz, (g + 1) * gsz)
            y = y_intra[g] + y_state[g] * ea_l[g]
            y = y + xbc_ref[rows, gsl] * dskip_ref[:, gsl]
            yz = y * _silu(zd_ref[rows, gsl])
            o_ref[rows, gsl] = _rms(yz, ng_ref[:, gsl]).astype(o_ref.dtype)
        return carry

    lax.fori_loop(0, tb // CHUNK, chunk, 0, unroll=4)


def _ssd_scan(xbc, zd, a_log_row, dt_bias_row, d_skip_row, norm_g, *, batch, seq, tb):
    t = xbc.shape[0]
    nt = seq // tb
    return pl.pallas_call(
        functools.partial(_ssd_scan_kernel, tb=tb),
        grid=(batch, nt),
        in_specs=[pl.BlockSpec((tb, SSD_CONV_DIM), lambda b, i: (b * nt + i, 0)),
                  pl.BlockSpec((tb, SSD_INNER + LANES), lambda b, i: (b * nt + i, 0)),
                  _resident((1, LANES)), _resident((1, LANES)),
                  _resident((1, SSD_INNER)), _resident((1, SSD_INNER))],
        out_specs=pl.BlockSpec((tb, SSD_INNER), lambda b, i: (b * nt + i, 0)),
        out_shape=jax.ShapeDtypeStruct((t, SSD_INNER), BF16),
        scratch_shapes=[pltpu.VMEM((SSD_GROUPS, SSD_STATE, SSD_HPG * SSD_HEADDIM), F32)],
        compiler_params=_cparams("parallel", "arbitrary"),
        name="ssd_scan",
    )(xbc, zd, a_log_row, dt_bias_row, d_skip_row, norm_g)


def _rw_proj_kernel(x_ref, xh_ref, gpre_ref, mu_ref, wrkv_ref, w0_ref, w1_ref, w2_ref, a0_ref, a1_ref, a2_ref,
                    g1_ref, g2_ref, kk_ref, ka_ref,
                    r_out, lw_out, k_out, v_out, kk_out, a_out, gate_out, *, tm, tiles_per_seq):
    i = pl.program_id(0)
    g = gpre_ref[...]
    first = (i % tiles_per_seq) == 0
    x = _rms(x_ref[...], g)
    xh = _rms(jnp.where(first, 0.0, xh_ref[...]), g)
    prev = jnp.concatenate([xh, x], axis=0)[HALO - 1:HALO - 1 + tm]
    xx = prev - x
    mu = mu_ref[...]
    xr, xw, xk, xv, xa, xg = [(x + xx * mu[j:j + 1]) for j in range(6)]
    r = _mm(xr, wrkv_ref[0])
    k = _mm(xk, wrkv_ref[1])
    v = _mm(xv, wrkv_ref[2])
    w = -_softplus(-(w0_ref[...] + _mm(jnp.tanh(_mm(xw, w1_ref[...])), w2_ref[...]))) - 0.5
    a = _sigmoid(a0_ref[...] + _mm(_mm(xa, a1_ref[...]), a2_ref[...]))
    gate = _mm(_sigmoid(_mm(xg, g1_ref[...])), g2_ref[...])
    r_out[...] = r.astype(r_out.dtype)
    lw_out[...] = -jnp.exp(w)
    k_out[...] = (k * (1.0 + (a - 1.0) * ka_ref[...])).astype(k_out.dtype)
    v_out[...] = v.astype(v_out.dtype)
    kk_out[...] = (k * kk_ref[...]).astype(kk_out.dtype)
    a_out[...] = a.astype(a_out.dtype)
    gate_out[...] = gate.astype(gate_out.dtype)


def _rw_proj(h, g_pre, mu, w_rkv, w0, w1, w2, a0, a1, a2, g1, g2, k_k, k_a, *, seq, tm):
    t, d = h.shape
    hb = tm // HALO
    row = lambda: _resident((1, d))
    tok = pl.BlockSpec((tm, d), lambda i: (i, 0))
    outs = [jax.ShapeDtypeStruct((t, d), F32 if j == 1 else BF16) for j in range(7)]
    return pl.pallas_call(
        functools.partial(_rw_proj_kernel, tm=tm, tiles_per_seq=seq // tm),
        grid=(t // tm,),
        in_specs=[tok, pl.BlockSpec((HALO, d), lambda i: (jnp.maximum(i * hb - 1, 0), 0)),
                  row(), _resident(mu.shape), _resident(w_rkv.shape), row(), _resident(w1.shape),
                  _resident(w2.shape), row(), _resident(a1.shape), _resident(a2.shape),
                  _resident(g1.shape), _resident(g2.shape), row(), row()],
        out_specs=[tok] * 7,
        out_shape=outs,
        compiler_params=_cparams("parallel"),
        name="rw_proj",
    )(h, h, g_pre, mu, w_rkv, w0, w1, w2, a0, a1, a2, g1, g2, k_k, k_a)


def _rw_scan_kernel(r_ref, lw_ref, k_ref, v_ref, kk_ref, a_ref, gate_ref, rk_ref, lng_ref, lnb_ref,
                    o_ref, s_ref, *, tb):
    @pl.when(pl.program_id(1) == 0)
    def _():
        s_ref[...] = jnp.zeros_like(s_ref)

    row, col = _tri_masks(CHUNK)
    lower_ones = (col <= row).astype(F32)
    rowp, colp, blk = _pair_masks(CHUNK)
    assert RW_HEAD == CHUNK and 2 * RW_HEAD == LANES
    incl_p = colp <= rowp
    strict_p = colp < rowp
    first = blk == 0
    keep1 = blk[:1].astype(F32)
    keep0 = 1.0 - keep1
    npairs = RW_HEADS // 2

    def pairs(t):
        return jnp.stack([t[:, p * LANES:(p + 1) * LANES] for p in range(npairs)])

    def flat(t):
        return jnp.concatenate([t[p] for p in range(npairs)], axis=-1)

    def head_sum(t):
        s0 = jnp.sum(jnp.where(first, t, 0.0), axis=-1, keepdims=True)
        s1 = jnp.sum(jnp.where(first, 0.0, t), axis=-1, keepdims=True)
        return jnp.where(first, s0, s1)

    def to_pairs(rows0, rows1):
        bk0, bk1 = rows0[:, :, :LANES], rows1[:, :, :LANES]
        m_b = jnp.where(first, bk0, pltpu.roll(bk1, RW_HEAD, axis=2))
        m_k = jnp.where(first, pltpu.roll(bk0, RW_HEAD, axis=2), bk1)
        m_s = jnp.where(first, rows0[:, :, LANES:], rows1[:, :, LANES:])
        return m_b, m_k, m_s

    def chunk(c, carry):
        r0 = pl.multiple_of(c * CHUNK, CHUNK)
        rows = pl.ds(r0, CHUNK)
        lw = lw_ref[rows, :]
        wc = _cumsum_rows(lower_ones, lw)
        e_inv = jnp.exp(-wc)
        r = r_ref[rows, :].astype(F32)
        k = k_ref[rows, :].astype(F32)
        vp = pairs(v_ref[rows, :].astype(F32))
        kk = pairs(kk_ref[rows, :].astype(F32))
        kk = kk * lax.rsqrt(head_sum(kk * kk) + 1e-6)
        rt = pairs(r * jnp.exp(wc))
        kt = pairs(k * e_inv)
        at = -kk * pairs(jnp.exp(wc - lw))
        bt = kk * pairs(a_ref[rows, :].astype(F32) * e_inv)
        state = s_ref[...]
        ar = jnp.concatenate([at, rt], axis=1)
        p = _bmm(jnp.concatenate([ar * keep0, ar * keep1], axis=1),
                 jnp.concatenate([bt, kt, state, state], axis=1), _NT)
        a_ab, a_ak, a_s = to_pairs(p[:, :CHUNK], p[:, 2 * CHUNK:3 * CHUNK])
        r_b, r_k, r_s = to_pairs(p[:, CHUNK:2 * CHUNK], p[:, 3 * CHUNK:])
        t_inv = _unit_lower_inverse_pairs(jnp.where(strict_p, -a_ab, 0.0), rowp, colp, blk)
        v_bd = _block_diag(vp, blk)
        u = _bmm(t_inv, _block_diag(a_s + _bmm(jnp.where(strict_p, a_ak, 0.0), v_bd), blk))
        m = jnp.concatenate([jnp.where(incl_p, r_b, 0.0), jnp.where(incl_p, r_k, 0.0)], axis=2)
        y = r_s + _bmm(m, jnp.concatenate([_block_diag(u, blk), v_bd], axis=1))
        prod = _bmm(jnp.concatenate([u, vp], axis=1), jnp.concatenate([bt, kt], axis=1), _TN)
        upd = jnp.where(first, prod[:, :RW_HEAD], prod[:, RW_HEAD:])
        s_ref[...] = (state + upd) * pairs(jnp.exp(wc[CHUNK - 1:CHUNK, :]))
        yc = y - head_sum(y) * (1.0 / RW_HEAD)
        yn = yc * lax.rsqrt(head_sum(yc * yc) * (1.0 / RW_HEAD) + RW_GN_EPS)
        bonus = head_sum(pairs(r * k * rk_ref[...])) * vp
        out = (flat(yn) * lng_ref[...] + lnb_ref[...] + flat(bonus)) * gate_ref[rows, :].astype(F32)
        o_ref[rows, :] = out.astype(o_ref.dtype)
        return carry

    lax.fori_loop(0, tb // CHUNK, chunk, 0, unroll=2)


def _rw_scan(r, lw, k, v, kk, a, gate, r_k, ln_g, ln_b, *, batch, seq, tb):
    t, d = r.shape
    nt = seq // tb
    tok = pl.BlockSpec((tb, d), lambda bi, i: (bi * nt + i, 0))
    return pl.pallas_call(
        functools.partial(_rw_scan_kernel, tb=tb),
        grid=(batch, nt),
        in_specs=[tok] * 7 + [_resident((1, d))] * 3,
        out_specs=tok,
        out_shape=jax.ShapeDtypeStruct((t, d), BF16),
        scratch_shapes=[pltpu.VMEM((RW_HEADS // 2, RW_HEAD, 2 * RW_HEAD), F32)],
        compiler_params=_cparams("parallel", "arbitrary"),
        name="rw_scan",
    )(r, lw, k, v, kk, a, gate, r_k, ln_g, ln_b)


def _row(v):
    return v.reshape(1, -1).astype(F32)


def _lane_row(v, offset):
    return jnp.zeros((1, LANES), F32).at[0, offset:offset + v.shape[0]].set(v.astype(F32))


def _pad_cols(w, n):
    return jnp.pad(w, ((0, 0), (0, n - w.shape[1])))


def _tile(seq, want):
    return min(seq, want)


def _deltanet_layer(h, g_pre, g_post, w_in, conv_w, a_log, dt_bias, norm_g, w_out, *, batch, seq):
    w_conv = w_in[:, :DN_CONV_DIM].astype(BF16)
    w_rest = _pad_cols(w_in[:, DN_CONV_DIM:], DN_VAL + LANES).astype(BF16)
    qkv = _norm_proj_conv(h, g_pre, w_conv, conv_w, jnp.zeros((1, DN_CONV_DIM), F32),
                          seq=seq, tm=_tile(seq, 512))
    zb = _norm_proj(h, g_pre, w_rest, tm=_tile(seq, 512))
    o = _dn_scan(qkv, zb, _lane_row(a_log, DN_V_HEADS), _lane_row(dt_bias, DN_V_HEADS), _row(norm_g),
                 batch=batch, seq=seq, tb=_tile(seq, 512))
    return _proj_post(o, h, w_out.astype(BF16), g_post, tm=_tile(seq, 512))


def _ssd_layer(h, g_pre, g_post, w_in, conv_w, conv_b, a_log, dt_bias, d_skip, norm_g, w_out, *, batch, seq):
    w_conv = w_in[:, SSD_INNER:SSD_INNER + SSD_CONV_DIM].astype(BF16)
    w_rest = _pad_cols(jnp.concatenate([w_in[:, :SSD_INNER], w_in[:, SSD_INNER + SSD_CONV_DIM:]], axis=1),
                       SSD_INNER + LANES).astype(BF16)
    xbc = _norm_proj_conv(h, g_pre, w_conv, conv_w, _row(conv_b), seq=seq, tm=_tile(seq, 512))
    zd = _norm_proj(h, g_pre, w_rest, tm=_tile(seq, 512))
    y = _ssd_scan(xbc, zd, _lane_row(a_log, 0), _lane_row(dt_bias, 0),
                  _row(jnp.repeat(d_skip, SSD_HEADDIM)), _row(norm_g), batch=batch, seq=seq, tb=_tile(seq, 512))
    return _proj_post(y, h, w_out.astype(BF16), g_post, tm=_tile(seq, 512))


def _rwkv_layer(h, g_pre, g_post, mu, w_rkv, w0, w1, w2, a0, a1, a2, g1, g2, k_k, k_a, r_k, ln_g, ln_b, w_out,
                *, batch, seq):
    bf = lambda w: w.astype(BF16)
    outs = _rw_proj(h, g_pre, mu.astype(F32), bf(w_rkv), _row(w0), bf(w1), bf(w2), _row(a0), bf(a1), bf(a2),
                    bf(g1), bf(g2), _row(k_k), _row(k_a), seq=seq, tm=_tile(seq, 256))
    y = _rw_scan(*outs, _row(r_k), _row(ln_g), _row(ln_b), batch=batch, seq=seq, tb=_tile(seq, 512))
    return _proj_post(y, h, bf(w_out), g_post, tm=_tile(seq, 512))


def kernel(x, mem, sandwich_g, ffn_w_in, ffn_w_out, mem_norm_g, xa_w_q, xa_w_kv, xa_w_o, dn_w_in, dn_conv_w, dn_a_log, dn_dt_bias, dn_norm_g, dn_w_out, ssd_w_in, ssd_conv_w, ssd_conv_b, ssd_a_log, ssd_dt_bias, ssd_d, ssd_norm_g, ssd_w_out, rw_mu, rw_w_rkv, rw_w0, rw_w1, rw_w2, rw_a0, rw_a1, rw_a2, rw_g1, rw_g2, rw_k_k, rw_k_a, rw_r_k, rw_ln_g, rw_ln_b, rw_w_out):
    batch, seq, d = x.shape
    n_mem = mem.shape[1]
    h = x.reshape(batch * seq, d)
    mem2 = mem.reshape(batch * n_mem, d)
    tm_ffn = _tile(seq, 512)

    def ffn(h, l, which, g_pre, g_post):
        return _ffn(h, _row(g_pre), _row(g_post), ffn_w_in[l, which].astype(BF16),
                    ffn_w_out[l, which].astype(BF16), tm=tm_ffn)

    for l in range(DEPTH):
        g = sandwich_g[l]
        kind = l % N_MIXERS
        j = l // N_MIXERS
        h = ffn(h, l, 0, g[0, 0], g[0, 1])
        gp, gq = _row(g[1, 0]), _row(g[1, 1])
        if kind == 0:
            h = _deltanet_layer(h, gp, gq, dn_w_in[j], dn_conv_w[j], dn_a_log[j], dn_dt_bias[j], dn_norm_g[j],
                                dn_w_out[j], batch=batch, seq=seq)
        elif kind == 1:
            h = _ssd_layer(h, gp, gq, ssd_w_in[j], ssd_conv_w[j], ssd_conv_b[j], ssd_a_log[j], ssd_dt_bias[j],
                           ssd_d[j], ssd_norm_g[j], ssd_w_out[j], batch=batch, seq=seq)
        else:
            h = _rwkv_layer(h, gp, gq, rw_mu[j], rw_w_rkv[j], rw_w0[j], rw_w1[j], rw_w2[j], rw_a0[j], rw_a1[j],
                            rw_a2[j], rw_g1[j], rw_g2[j], rw_k_k[j], rw_k_a[j], rw_r_k[j].reshape(-1),
                            rw_ln_g[j], rw_ln_b[j], rw_w_out[j], batch=batch, seq=seq)
        kv = _norm_proj(mem2, _row(mem_norm_g[l]), xa_w_kv[l].astype(BF16), tm=n_mem, out_dtype=BF16)
        h = _xattn(h, kv, _row(g[2, 0]), _row(g[2, 1]), xa_w_q[l].astype(BF16), xa_w_o[l].astype(BF16),
                   seq=seq, tm=_tile(seq, 512))
        h = ffn(h, l, 1, g[3, 0], g[3, 1])
    return h.reshape(batch, seq, d)
```

```python
import functools

import jax
import jax.numpy as jnp
from jax import lax
from jax.experimental import pallas as pl
from jax.experimental.pallas import tpu as pltpu

F32 = jnp.float32
BF16 = jnp.bfloat16

D_MODEL = 1024
DEPTH = 4
N_MIXERS = 3
CHUNK = 64
CONV_W = 4
RMS_EPS = 1e-6
MACARON_W = 0.5
D_FF = 2816

DN_QK_HEADS = 8
DN_V_HEADS = 16
DN_DK = 128
DN_DV = 128
DN_Q = DN_QK_HEADS * DN_DK
DN_VAL = DN_V_HEADS * DN_DV
DN_CONV_DIM = 2 * DN_Q + DN_VAL

SSD_INNER = 2 * D_MODEL
SSD_HEADDIM = 64
SSD_HEADS = SSD_INNER // SSD_HEADDIM
SSD_GROUPS = 8
SSD_HPG = SSD_HEADS // SSD_GROUPS
SSD_STATE = 128
SSD_BC = SSD_GROUPS * SSD_STATE
SSD_CONV_DIM = SSD_INNER + 2 * SSD_BC

RW_HEAD = 64
RW_HEADS = D_MODEL // RW_HEAD
RW_GN_EPS = 64e-5

XA_HEADS = 4
XA_DH = D_MODEL // XA_HEADS

LANES = 128
SUBLANES = 8
HALO = 16
VMEM_LIMIT = 56 * 1024 * 1024
TOKEN_TILE = 512
RW_PROJ_TILE = 256
SCAN_TILE = 512
HI = lax.Precision.HIGHEST


def _cparams(*sem):
    return pltpu.CompilerParams(dimension_semantics=sem, vmem_limit_bytes=VMEM_LIMIT)


def _resident(shape):
    nd = len(shape)
    return pl.BlockSpec(shape, lambda *_: (0,) * nd, pipeline_mode=pl.Buffered(1))


def _rms(x, g, eps=RMS_EPS):
    return x * lax.rsqrt(jnp.mean(x * x, axis=-1, keepdims=True) + eps) * g


def _sigmoid(x):
    return 0.5 * jnp.tanh(0.5 * x) + 0.5


def _half_silu(h):
    return h + h * jnp.tanh(h)


def _silu(x):
    return _half_silu(0.5 * x)


def _softplus(x):
    return jnp.maximum(x, 0.0) + jnp.log1p(jnp.exp(-jnp.abs(x)))


def _mm(a, b):
    return jnp.dot(a.astype(BF16), b.astype(BF16), preferred_element_type=F32)


def _mm_nt(a, b):
    return lax.dot_general(a.astype(BF16), b.astype(BF16), (((1,), (1,)), ((), ())),
                           preferred_element_type=F32)


def _mm_hi(a, b):
    return jnp.dot(a, b, preferred_element_type=F32, precision=HI)


def _tri_masks(n):
    row = lax.broadcasted_iota(jnp.int32, (n, n), 0)
    col = lax.broadcasted_iota(jnp.int32, (n, n), 1)
    return row, col


_NN = ((2,), (1,))
_NT = ((2,), (2,))
_TN = ((1,), (1,))


def _bmm(a, b, dims=_NN):
    return lax.dot_general(a.astype(BF16), b.astype(BF16), (dims, ((0,), (0,))), preferred_element_type=F32)


def _pair_masks(n):
    row = lax.broadcasted_iota(jnp.int32, (n, 2 * n), 0)
    lane = lax.broadcasted_iota(jnp.int32, (n, 2 * n), 1)
    return row, lane & (n - 1), lane // n


def _block_diag(p, blk):
    pb = p.astype(BF16)
    keep1 = blk.astype(BF16)
    return jnp.concatenate([pb * (1 - keep1), pb * keep1], axis=1)


def _unit_lower_inverse_pairs(a, row, col, blk):
    n = a.shape[1]
    eye = (row == col).astype(F32)
    x = eye - jnp.where((row >> 1) == (col >> 1), a, 0.0)
    b = 2
    while b < n:
        sh = b.bit_length() - 1
        m = ((row >> (sh + 1)) == (col >> (sh + 1))) & (((row >> sh) & 1) == 1) & (((col >> sh) & 1) == 0)
        lm = jnp.where(m, a, 0.0)
        x = x - _bmm(_bmm(x, _block_diag(lm, blk)), _block_diag(x, blk))
        b *= 2
    return x


def _cumsum_rows(lower_ones, v):
    return _mm_hi(lower_ones, v)


def _norm_proj_kernel(x_ref, g_ref, w_ref, o_ref):
    xn = _rms(x_ref[...], g_ref[...]).astype(BF16)
    o_ref[...] = jnp.dot(xn, w_ref[...], preferred_element_type=F32).astype(o_ref.dtype)


def _norm_proj(x, g, w, *, tm, out_dtype=F32):
    t, d = x.shape
    n = w.shape[1]
    return pl.pallas_call(
        _norm_proj_kernel,
        grid=(t // tm,),
        in_specs=[pl.BlockSpec((tm, d), lambda i: (i, 0)), _resident((1, d)), _resident((d, n))],
        out_specs=pl.BlockSpec((tm, n), lambda i: (i, 0)),
        out_shape=jax.ShapeDtypeStruct((t, n), out_dtype),
        compiler_params=_cparams("parallel"),
        name="norm_proj",
    )(x, g, w)


def _norm_proj_conv_kernel(x_ref, xh_ref, g_ref, w_ref, cw_ref, cb_ref, o_ref, p_ref, *, tm, tn, tiles_per_seq):
    i = pl.program_id(0)
    g = g_ref[...]
    first = (i % tiles_per_seq) == 0
    xh = jnp.where(first, 0.0, xh_ref[...])
    xn = jnp.concatenate([_rms(xh, g), _rms(x_ref[...], g)], axis=0).astype(BF16)
    n = w_ref.shape[1]
    for c in range(n // tn):
        sl = slice(c * tn, (c + 1) * tn)
        p_ref[c % 2] = jnp.dot(xn, w_ref[:, sl], preferred_element_type=F32)
        cw = 0.5 * cw_ref[:, sl]
        hy = 0.5 * cb_ref[:, sl]
        for j in range(CONV_W):
            hy = hy + cw[j:j + 1] * p_ref[c % 2, pl.ds(HALO - (CONV_W - 1) + j, tm), :]
        o_ref[:, sl] = _half_silu(hy)


def _norm_proj_conv(x, g, w, conv_w, conv_b, *, seq, tm, tn=512):
    t, d = x.shape
    n = w.shape[1]
    hb = tm // HALO
    kern = functools.partial(_norm_proj_conv_kernel, tm=tm, tn=tn, tiles_per_seq=seq // tm)
    return pl.pallas_call(
        kern,
        grid=(t // tm,),
        in_specs=[pl.BlockSpec((tm, d), lambda i: (i, 0)),
                  pl.BlockSpec((HALO, d), lambda i: (jnp.maximum(i * hb - 1, 0), 0)),
                  _resident((1, d)), _resident((d, n)), _resident((CONV_W, n)), _resident((1, n))],
        out_specs=pl.BlockSpec((tm, n), lambda i: (i, 0)),
        out_shape=jax.ShapeDtypeStruct((t, n), F32),
        scratch_shapes=[pltpu.VMEM((2, HALO + tm, tn), F32)],
        compiler_params=_cparams("parallel"),
        name="norm_proj_conv",
    )(x, x, g, w, conv_w, conv_b)


def _proj_post_kernel(a_ref, h_ref, w_ref, g_ref, o_ref, *, weight):
    u = jnp.dot(a_ref[...], w_ref[...], preferred_element_type=F32)
    o_ref[...] = h_ref[...] + weight * _rms(u, g_ref[...])


def _proj_post(a, h, w, g_post, *, tm, weight=1.0):
    t, k = a.shape
    d = w.shape[1]
    return pl.pallas_call(
        functools.partial(_proj_post_kernel, weight=weight),
        grid=(t // tm,),
        in_specs=[pl.BlockSpec((tm, k), lambda i: (i, 0)), pl.BlockSpec((tm, d), lambda i: (i, 0)),
                  _resident((k, d)), _resident((1, d))],
        out_specs=pl.BlockSpec((tm, d), lambda i: (i, 0)),
        out_shape=jax.ShapeDtypeStruct((t, d), F32),
        compiler_params=_cparams("parallel"),
        name="proj_post",
    )(a, h, w, g_post)


def _ffn_kernel(h_ref, gpre_ref, gpost_ref, wg_ref, wu_ref, wo_ref, o_ref, acc_ref, *, tf):
    h = h_ref[...]
    xn = _rms(h, gpre_ref[...]).astype(BF16)
    nf = wg_ref.shape[1] // tf
    for c in range(nf):
        sl = slice(c * tf, (c + 1) * tf)
        gate = jnp.dot(xn, wg_ref[:, sl], preferred_element_type=F32)
        up = jnp.dot(xn, wu_ref[:, sl], preferred_element_type=F32)
        act = (_silu(gate) * up).astype(BF16)
        part = jnp.dot(act, wo_ref[sl, :], preferred_element_type=F32)
        if c == 0:
            acc_ref[...] = part
        else:
            acc_ref[...] += part
    o_ref[...] = h + MACARON_W * _rms(acc_ref[...], gpost_ref[...])


def _ffn(h, g_pre, g_post, w_in, w_out, *, tm, tf=256):
    t, d = h.shape
    f = w_out.shape[0]
    half = lambda j: pl.BlockSpec((d, f), lambda i: (0, j), pipeline_mode=pl.Buffered(1))
    return pl.pallas_call(
        functools.partial(_ffn_kernel, tf=tf),
        grid=(t // tm,),
        in_specs=[pl.BlockSpec((tm, d), lambda i: (i, 0)), _resident((1, d)), _resident((1, d)),
                  half(0), half(1), _resident((f, d))],
        out_specs=pl.BlockSpec((tm, d), lambda i: (i, 0)),
        out_shape=jax.ShapeDtypeStruct((t, d), F32),
        scratch_shapes=[pltpu.VMEM((tm, d), F32)],
        compiler_params=_cparams("parallel"),
        name="ffn",
    )(h, g_pre, g_post, w_in, w_in, w_out)


def _xattn_kernel(h_ref, kv_ref, gpre_ref, gpost_ref, wq_ref, wo_ref, o_ref):
    h = h_ref[...]
    d = h.shape[1]
    xn = _rms(h, gpre_ref[...]).astype(BF16)
    q = jnp.dot(xn, wq_ref[...], preferred_element_type=F32) * XA_DH ** -0.5
    outs = []
    for hh in range(XA_HEADS):
        sl = slice(hh * XA_DH, (hh + 1) * XA_DH)
        k = kv_ref[:, sl]
        v = kv_ref[:, d + hh * XA_DH:d + (hh + 1) * XA_DH]
        s = _mm_nt(q[:, sl], k)
        s = s - jnp.max(s, axis=-1, keepdims=True)
        e = jnp.exp(s)
        p = e / jnp.sum(e, axis=-1, keepdims=True)
        outs.append(_mm(p, v))
    o = jnp.concatenate(outs, axis=-1).astype(BF16)
    u = jnp.dot(o, wo_ref[...], preferred_element_type=F32)
    o_ref[...] = h + _rms(u, gpost_ref[...])


def _xattn(h, kv, g_pre, g_post, w_q, w_o, *, seq, tm):
    t, d = h.shape
    n_mem = kv.shape[0] // (t // seq)
    tiles_per_seq = seq // tm
    return pl.pallas_call(
        _xattn_kernel,
        grid=(t // tm,),
        in_specs=[pl.BlockSpec((tm, d), lambda i: (i, 0)),
                  pl.BlockSpec((n_mem, 2 * d), lambda i: (i // tiles_per_seq, 0)),
                  _resident((1, d)), _resident((1, d)), _resident((d, d)), _resident((d, d))],
        out_specs=pl.BlockSpec((tm, d), lambda i: (i, 0)),
        out_shape=jax.ShapeDtypeStruct((t, d), F32),
        compiler_params=_cparams("parallel"),
        name="xattn",
    )(h, kv, g_pre, g_post, w_q, w_o)


def _dn_scan_kernel(qkv_ref, zb_ref, alog_ref, dtb_ref, ng_ref, o_ref, s_ref, *, tb):
    @pl.when(pl.program_id(1) == 0)
    def _():
        s_ref[...] = jnp.zeros_like(s_ref)

    row, col = _tri_masks(CHUNK)
    lower_ones = (col <= row).astype(F32)
    rowp, colp, blk = _pair_masks(CHUNK)
    incl_p = colp <= rowp
    strict_p = colp < rowp
    rep = DN_V_HEADS // DN_QK_HEADS
    assert rep == 2
    neg_a = -jnp.exp(alog_ref[...])
    dtb = dtb_ref[...]
    ng = ng_ref[...]

    def chunk(c, carry):
        r0 = pl.multiple_of(c * CHUNK, CHUNK)
        rows = pl.ds(r0, CHUNK)
        ba = zb_ref[rows, DN_VAL:DN_VAL + LANES]
        beta_all = _sigmoid(ba)
        g_all = neg_a * _softplus(ba + dtb)
        gc = _cumsum_rows(lower_ones, g_all)
        gct = gc.T
        eg_all = jnp.exp(gc)
        glast = gc[CHUNK - 1:CHUNK, :]
        eend_all = jnp.exp(glast - gc)
        elast_all = jnp.exp(glast)
        qs, ks = [], []
        for hq in range(DN_QK_HEADS):
            q = qkv_ref[rows, hq * DN_DK:(hq + 1) * DN_DK]
            k = qkv_ref[rows, DN_Q + hq * DN_DK:DN_Q + (hq + 1) * DN_DK]
            qs.append(q * lax.rsqrt(jnp.sum(q * q, axis=-1, keepdims=True) + 1e-6) * DN_DK ** -0.5)
            ks.append(k * lax.rsqrt(jnp.sum(k * k, axis=-1, keepdims=True) + 1e-6))
        k8 = jnp.stack(ks)
        qkk = _bmm(jnp.stack([jnp.concatenate([q, k], axis=0) for q, k in zip(qs, ks)]),
                   jnp.concatenate([k8, k8], axis=1), _NT)
        qk_p, kk_p = qkk[:, :CHUNK], qkk[:, CHUNK:]
        a_l, att_l = [], []
        for hq in range(DN_QK_HEADS):
            h0 = hq * rep
            beta_p = jnp.take_along_axis(beta_all, blk + h0, axis=1)
            g_col = jnp.take_along_axis(gc, blk + (DN_V_HEADS + h0), axis=1)
            g_row = jnp.concatenate([gct[DN_V_HEADS + h0 + e:DN_V_HEADS + h0 + e + 1, :] for e in range(rep)], axis=1)
            decay = jnp.where(incl_p, jnp.exp(g_col - g_row), 0.0)
            a_l.append(jnp.where(strict_p, kk_p[hq] * decay * beta_p, 0.0))
            att_l.append(qk_p[hq] * decay)
        t_inv = _unit_lower_inverse_pairs(jnp.stack(a_l), rowp, colp, blk)
        t_l, rhs_l, qe_l, kend_l, elast_l = [], [], [], [], []
        for h in range(DN_V_HEADS):
            hq, e = divmod(h, rep)
            gl = DN_V_HEADS + h
            beta = beta_all[:, h:h + 1]
            eg = eg_all[:, gl:gl + 1]
            v = qkv_ref[rows, 2 * DN_Q + h * DN_DV:2 * DN_Q + (h + 1) * DN_DV]
            t_l.append(t_inv[hq][:, e * CHUNK:(e + 1) * CHUNK])
            rhs_l.append(jnp.concatenate([v * beta, ks[hq] * (beta * eg)], axis=-1))
            qe_l.append(qs[hq] * eg)
            kend_l.append(ks[hq] * eend_all[:, gl:gl + 1])
            elast_l.append(elast_all[:, gl:gl + 1])
        sol = _bmm(jnp.stack(t_l), jnp.stack(rhs_l))
        u, wk = sol[:, :, :DN_DV], sol[:, :, DN_DV:]
        state = s_ref[...]
        ws = _bmm(jnp.concatenate([wk, jnp.stack(qe_l)], axis=1), state)
        v_new = u - ws[:, :CHUNK]
        zero = jnp.zeros((CHUNK, DN_DV), F32)
        vn_bd = jnp.stack([jnp.concatenate([jnp.concatenate([v_new[hq * rep], zero], axis=1),
                                            jnp.concatenate([zero, v_new[hq * rep + 1]], axis=1)], axis=0)
                           for hq in range(DN_QK_HEADS)])
        o_p = _bmm(jnp.stack(att_l), vn_bd)
        s_ref[...] = state * jnp.stack(elast_l) + _bmm(jnp.stack(kend_l), v_new, _TN)
        for h in range(DN_V_HEADS):
            hq, e = divmod(h, rep)
            vsl = slice(h * DN_DV, (h + 1) * DN_DV)
            o = ws[h, CHUNK:] + o_p[hq][:, e * DN_DV:(e + 1) * DN_DV]
            z = zb_ref[rows, vsl]
            o_ref[rows, vsl] = (_rms(o, ng) * _silu(z)).astype(o_ref.dtype)
        return carry

    lax.fori_loop(0, tb // CHUNK, chunk, 0, unroll=2)


def _dn_scan(qkv, zb, a_log_row, dt_bias_row, norm_g, *, batch, seq, tb):
    t = qkv.shape[0]
    nt = seq // tb
    return pl.pallas_call(
        functools.partial(_dn_scan_kernel, tb=tb),
        grid=(batch, nt),
        in_specs=[pl.BlockSpec((tb, DN_CONV_DIM), lambda b, i: (b * nt + i, 0)),
                  pl.BlockSpec((tb, DN_VAL + LANES), lambda b, i: (b * nt + i, 0)),
                  _resident((1, LANES)), _resident((1, LANES)), _resident((1, DN_DV))],
        out_specs=pl.BlockSpec((tb, DN_VAL), lambda b, i: (b * nt + i, 0)),
        out_shape=jax.ShapeDtypeStruct((t, DN_VAL), BF16),
        scratch_shapes=[pltpu.VMEM((DN_V_HEADS, DN_DK, DN_DV), F32)],
        compiler_params=_cparams("parallel", "arbitrary"),
        name="dn_scan",
    )(qkv, zb, a_log_row, dt_bias_row, norm_g)


def _ssd_scan_kernel(xbc_ref, zd_ref, alog_ref, dtb_ref, dskip_ref, ng_ref, o_ref, s_ref, *, tb):
    @pl.when(pl.program_id(1) == 0)
    def _():
        s_ref[...] = jnp.zeros_like(s_ref)

    row, col = _tri_masks(CHUNK)
    lower_ones = (col <= row).astype(F32)
    neg_a = -jnp.exp(alog_ref[...])
    dtb = dtb_ref[...]
    gsz = SSD_HPG * SSD_HEADDIM
    lane = lax.broadcasted_iota(jnp.int32, (CHUNK, LANES), 1)
    pair = lane // SSD_HEADDIM
    rowg = lax.broadcasted_iota(jnp.int32, (CHUNK, gsz), 0)
    laneg = lax.broadcasted_iota(jnp.int32, (CHUNK, gsz), 1)
    incl_g = (laneg & (SSD_HEADDIM - 1)) <= rowg
    head_g = laneg // SSD_HEADDIM

    def spread(cols, g):
        halves = [jnp.take_along_axis(cols, pair + (g * SSD_HPG + 2 * p), axis=1) for p in range(2)]
        return jnp.concatenate(halves, axis=1)

    def chunk(c, carry):
        r0 = pl.multiple_of(c * CHUNK, CHUNK)
        rows = pl.ds(r0, CHUNK)
        dt_all = _softplus(zd_ref[rows, SSD_INNER:SSD_INNER + LANES] + dtb)
        ac = _cumsum_rows(lower_ones, dt_all * neg_a)
        act = ac.T
        bm = jnp.stack([xbc_ref[rows, SSD_INNER + g * SSD_STATE:SSD_INNER + (g + 1) * SSD_STATE]
                        for g in range(SSD_GROUPS)])
        cm = jnp.stack([xbc_ref[rows, SSD_INNER + SSD_BC + g * SSD_STATE:SSD_INNER + SSD_BC + (g + 1) * SSD_STATE]
                        for g in range(SSD_GROUPS)])
        cb = _bmm(cm, jnp.concatenate([bm] * SSD_HPG, axis=1), _NT)
        state = s_ref[...]
        y_state = _bmm(cm, state)
        w_l, blk_l, xe_l, el_l, ea_l = [], [], [], [], []
        for g in range(SSD_GROUPS):
            xg = xbc_ref[rows, g * gsz:(g + 1) * gsz]
            ac_col = spread(ac, g)
            ac_row = jnp.concatenate([act[g * SSD_HPG + e:g * SSD_HPG + e + 1, :] for e in range(SSD_HPG)], axis=1)
            a_last = ac_col[CHUNK - 1:CHUNK, :]
            xdt = xg * spread(dt_all, g)
            w_l.append(cb[g] * jnp.where(incl_g, jnp.exp(ac_col - ac_row), 0.0))
            blk_l.append(jnp.concatenate([jnp.where(head_g == e, xdt, 0.0) for e in range(SSD_HPG)], axis=0))
            xe_l.append(xdt * jnp.exp(a_last - ac_col))
            el_l.append(jnp.exp(a_last))
            ea_l.append(jnp.exp(ac_col))
        y_intra = _bmm(jnp.stack(w_l), jnp.stack(blk_l))
        s_ref[...] = state * jnp.stack(el_l) + _bmm(bm, jnp.stack(xe_l), _TN)
        for g in range(SSD_GROUPS):
            gsl = slice(g * gsz, (g + 1) * gsz)
            y = y_intra[g] + y_state[g] * ea_l[g]
            y = y + xbc_ref[rows, gsl] * dskip_ref[:, gsl]
            yz = y * _silu(zd_ref[rows, gsl])
            o_ref[rows, gsl] = _rms(yz, ng_ref[:, gsl]).astype(o_ref.dtype)
        return carry

    lax.fori_loop(0, tb // CHUNK, chunk, 0, unroll=4)


def _ssd_scan(xbc, zd, a_log_row, dt_bias_row, d_skip_row, norm_g, *, batch, seq, tb):
    t = xbc.shape[0]
    nt = seq // tb
    return pl.pallas_call(
        functools.partial(_ssd_scan_kernel, tb=tb),
        grid=(batch, nt),
        in_specs=[pl.BlockSpec((tb, SSD_CONV_DIM), lambda b, i: (b * nt + i, 0)),
                  pl.BlockSpec((tb, SSD_INNER + LANES), lambda b, i: (b * nt + i, 0)),
                  _resident((1, LANES)), _resident((1, LANES)),
                  _resident((1, SSD_INNER)), _resident((1, SSD_INNER))],
        out_specs=pl.BlockSpec((tb, SSD_INNER), lambda b, i: (b * nt + i, 0)),
        out_shape=jax.ShapeDtypeStruct((t, SSD_INNER), BF16),
        scratch_shapes=[pltpu.VMEM((SSD_GROUPS, SSD_STATE, SSD_HPG * SSD_HEADDIM), F32)],
        compiler_params=_cparams("parallel", "arbitrary"),
        name="ssd_scan",
    )(xbc, zd, a_log_row, dt_bias_row, d_skip_row, norm_g)


def _rw_proj_kernel(x_ref, xh_ref, gpre_ref, mu_ref, wrkv_ref, w0_ref, w1_ref, w2_ref, a0_ref, a1_ref, a2_ref,
                    g1_ref, g2_ref, kk_ref, ka_ref,
                    r_out, lw_out, k_out, v_out, kk_out, a_out, gate_out, *, tm, tiles_per_seq):
    i = pl.program_id(0)
    g = gpre_ref[...]
    first = (i % tiles_per_seq) == 0
    x = _rms(x_ref[...], g)
    xh = _rms(jnp.where(first, 0.0, xh_ref[...]), g)
    prev = jnp.concatenate([xh, x], axis=0)[HALO - 1:HALO - 1 + tm]
    xx = prev - x
    mu = mu_ref[...]
    xr, xw, xk, xv, xa, xg = [(x + xx * mu[j:j + 1]) for j in range(6)]
    r = _mm(xr, wrkv_ref[0])
    k = _mm(xk, wrkv_ref[1])
    v = _mm(xv, wrkv_ref[2])
    w = -_softplus(-(w0_ref[...] + _mm(jnp.tanh(_mm(xw, w1_ref[...])), w2_ref[...]))) - 0.5
    a = _sigmoid(a0_ref[...] + _mm(_mm(xa, a1_ref[...]), a2_ref[...]))
    gate = _mm(_sigmoid(_mm(xg, g1_ref[...])), g2_ref[...])
    r_out[...] = r.astype(r_out.dtype)
    lw_out[...] = -jnp.exp(w)
    k_out[...] = (k * (1.0 + (a - 1.0) * ka_ref[...])).astype(k_out.dtype)
    v_out[...] = v.astype(v_out.dtype)
    kk_out[...] = (k * kk_ref[...]).astype(kk_out.dtype)
    a_out[...] = a.astype(a_out.dtype)
    gate_out[...] = gate.astype(gate_out.dtype)


def _rw_proj(h, g_pre, mu, w_rkv, w0, w1, w2, a0, a1, a2, g1, g2, k_k, k_a, *, seq, tm):
    t, d = h.shape
    hb = tm // HALO
    row = lambda: _resident((1, d))
    tok = pl.BlockSpec((tm, d), lambda i: (i, 0))
    outs = [jax.ShapeDtypeStruct((t, d), F32 if j == 1 else BF16) for j in range(7)]
    return pl.pallas_call(
        functools.partial(_rw_proj_kernel, tm=tm, tiles_per_seq=seq // tm),
        grid=(t // tm,),
        in_specs=[tok, pl.BlockSpec((HALO, d), lambda i: (jnp.maximum(i * hb - 1, 0), 0)),
                  row(), _resident(mu.shape), _resident(w_rkv.shape), row(), _resident(w1.shape),
                  _resident(w2.shape), row(), _resident(a1.shape), _resident(a2.shape),
                  _resident(g1.shape), _resident(g2.shape), row(), row()],
        out_specs=[tok] * 7,
        out_shape=outs,
        compiler_params=_cparams("parallel"),
        name="rw_proj",
    )(h, h, g_pre, mu, w_rkv, w0, w1, w2, a0, a1, a2, g1, g2, k_k, k_a)


def _rw_scan_kernel(r_ref, lw_ref, k_ref, v_ref, kk_ref, a_ref, gate_ref, rk_ref, lng_ref, lnb_ref,
                    o_ref, s_ref, *, tb):
    @pl.when(pl.program_id(1) == 0)
    def _():
        s_ref[...] = jnp.zeros_like(s_ref)

    row, col = _tri_masks(CHUNK)
    lower_ones = (col <= row).astype(F32)
    rowp, colp, blk = _pair_masks(CHUNK)
    assert RW_HEAD == CHUNK and 2 * RW_HEAD == LANES
    incl_p = colp <= rowp
    strict_p = colp < rowp
    first = blk == 0
    keep1 = blk[:1].astype(F32)
    keep0 = 1.0 - keep1
    npairs = RW_HEADS // 2

    def pairs(t):
        return jnp.stack([t[:, p * LANES:(p + 1) * LANES] for p in range(npairs)])

    def flat(t):
        return jnp.concatenate([t[p] for p in range(npairs)], axis=-1)

    def head_sum(t):
        s0 = jnp.sum(jnp.where(first, t, 0.0), axis=-1, keepdims=True)
        s1 = jnp.sum(jnp.where(first, 0.0, t), axis=-1, keepdims=True)
        return jnp.where(first, s0, s1)

    def to_pairs(rows0, rows1):
        bk0, bk1 = rows0[:, :, :LANES], rows1[:, :, :LANES]
        m_b = jnp.where(first, bk0, pltpu.roll(bk1, RW_HEAD, axis=2))
        m_k = jnp.where(first, pltpu.roll(bk0, RW_HEAD, axis=2), bk1)
        m_s = jnp.where(first, rows0[:, :, LANES:], rows1[:, :, LANES:])
        return m_b, m_k, m_s

    def chunk(c, carry):
        r0 = pl.multiple_of(c * CHUNK, CHUNK)
        rows = pl.ds(r0, CHUNK)
        lw = lw_ref[rows, :]
        wc = _cumsum_rows(lower_ones, lw)
        e_inv = jnp.exp(-wc)
        r = r_ref[rows, :].astype(F32)
        k = k_ref[rows, :].astype(F32)
        vp = pairs(v_ref[rows, :].astype(F32))
        kk = pairs(kk_ref[rows, :].astype(F32))
        kk = kk * lax.rsqrt(head_sum(kk * kk) + 1e-6)
        rt = pairs(r * jnp.exp(wc))
        kt = pairs(k * e_inv)
        at = -kk * pairs(jnp.exp(wc - lw))
        bt = kk * pairs(a_ref[rows, :].astype(F32) * e_inv)
        state = s_ref[...]
        ar = jnp.concatenate([at, rt], axis=1)
        p = _bmm(jnp.concatenate([ar * keep0, ar * keep1], axis=1),
                 jnp.concatenate([bt, kt, state, state], axis=1), _NT)
        a_ab, a_ak, a_s = to_pairs(p[:, :CHUNK], p[:, 2 * CHUNK:3 * CHUNK])
        r_b, r_k, r_s = to_pairs(p[:, CHUNK:2 * CHUNK], p[:, 3 * CHUNK:])
        t_inv = _unit_lower_inverse_pairs(jnp.where(strict_p, -a_ab, 0.0), rowp, colp, blk)
        v_bd = _block_diag(vp, blk)
        u = _bmm(t_inv, _block_diag(a_s + _bmm(jnp.where(strict_p, a_ak, 0.0), v_bd), blk))
        m = jnp.concatenate([jnp.where(incl_p, r_b, 0.0), jnp.where(incl_p, r_k, 0.0)], axis=2)
        y = r_s + _bmm(m, jnp.concatenate([_block_diag(u, blk), v_bd], axis=1))
        prod = _bmm(jnp.concatenate([u, vp], axis=1), jnp.concatenate([bt, kt], axis=1), _TN)
        upd = jnp.where(first, prod[:, :RW_HEAD], prod[:, RW_HEAD:])
        s_ref[...] = (state + upd) * pairs(jnp.exp(wc[CHUNK - 1:CHUNK, :]))
        yc = y - head_sum(y) * (1.0 / RW_HEAD)
        yn = yc * lax.rsqrt(head_sum(yc * yc) * (1.0 / RW_HEAD) + RW_GN_EPS)
        bonus = head_sum(pairs(r * k * rk_ref[...])) * vp
        out = (flat(yn) * lng_ref[...] + lnb_ref[...] + flat(bonus)) * gate_ref[rows, :].astype(F32)
        o_ref[rows, :] = out.astype(o_ref.dtype)
        return carry

    lax.fori_loop(0, tb // CHUNK, chunk, 0, unroll=2)


def _rw_scan(r, lw, k, v, kk, a, gate, r_k, ln_g, ln_b, *, batch, seq, tb):
    t, d = r.shape
    nt = seq // tb
    tok = pl.BlockSpec((tb, d), lambda bi, i: (bi * nt + i, 0))
    return pl.pallas_call(
        functools.partial(_rw_scan_kernel, tb=tb),
        grid=(batch, nt),
        in_specs=[tok] * 7 + [_resident((1, d))] * 3,
        out_specs=tok,
        out_shape=jax.ShapeDtypeStruct((t, d), BF16),
        scratch_shapes=[pltpu.VMEM((RW_HEADS // 2, RW_HEAD, 2 * RW_HEAD), F32)],
        compiler_params=_cparams("parallel", "arbitrary"),
        name="rw_scan",
    )(r, lw, k, v, kk, a, gate, r_k, ln_g, ln_b)


def _row(v):
    return v.reshape(1, -1).astype(F32)


def _lane_row(v, offset):
    return jnp.zeros((1, LANES), F32).at[0, offset:offset + v.shape[0]].set(v.astype(F32))


def _pad_cols(w, n):
    return jnp.pad(w, ((0, 0), (0, n - w.shape[1])))


def _tile(seq, want):
    tile = min(seq, want)
    assert seq % tile == 0 and tile % CHUNK == 0, (seq, tile)
    return tile


def _deltanet_layer(h, g_pre, g_post, w_in, conv_w, a_log, dt_bias, norm_g, w_out, *, batch, seq):
    w_conv = w_in[:, :DN_CONV_DIM].astype(BF16)
    w_rest = _pad_cols(w_in[:, DN_CONV_DIM:], DN_VAL + LANES).astype(BF16)
    qkv = _norm_proj_conv(h, g_pre, w_conv, conv_w, jnp.zeros((1, DN_CONV_DIM), F32),
                          seq=seq, tm=_tile(seq, TOKEN_TILE))
    zb = _norm_proj(h, g_pre, w_rest, tm=_tile(seq, TOKEN_TILE))
    o = _dn_scan(qkv, zb, _lane_row(a_log, DN_V_HEADS), _lane_row(dt_bias, DN_V_HEADS), _row(norm_g),
                 batch=batch, seq=seq, tb=_tile(seq, SCAN_TILE))
    return _proj_post(o, h, w_out.astype(BF16), g_post, tm=_tile(seq, TOKEN_TILE))


def _ssd_layer(h, g_pre, g_post, w_in, conv_w, conv_b, a_log, dt_bias, d_skip, norm_g, w_out, *, batch, seq):
    w_conv = w_in[:, SSD_INNER:SSD_INNER + SSD_CONV_DIM].astype(BF16)
    w_rest = _pad_cols(jnp.concatenate([w_in[:, :SSD_INNER], w_in[:, SSD_INNER + SSD_CONV_DIM:]], axis=1),
                       SSD_INNER + LANES).astype(BF16)
    xbc = _norm_proj_conv(h, g_pre, w_conv, conv_w, _row(conv_b), seq=seq, tm=_tile(seq, TOKEN_TILE))
    zd = _norm_proj(h, g_pre, w_rest, tm=_tile(seq, TOKEN_TILE))
    y = _ssd_scan(xbc, zd, _lane_row(a_log, 0), _lane_row(dt_bias, 0),
                  _row(jnp.repeat(d_skip, SSD_HEADDIM)), _row(norm_g), batch=batch, seq=seq, tb=_tile(seq, SCAN_TILE))
    return _proj_post(y, h, w_out.astype(BF16), g_post, tm=_tile(seq, TOKEN_TILE))


def _rwkv_layer(h, g_pre, g_post, mu, w_rkv, w0, w1, w2, a0, a1, a2, g1, g2, k_k, k_a, r_k, ln_g, ln_b, w_out,
                *, batch, seq):
    bf = lambda w: w.astype(BF16)
    outs = _rw_proj(h, g_pre, mu.astype(F32), bf(w_rkv), _row(w0), bf(w1), bf(w2), _row(a0), bf(a1), bf(a2),
                    bf(g1), bf(g2), _row(k_k), _row(k_a), seq=seq, tm=_tile(seq, RW_PROJ_TILE))
    y = _rw_scan(*outs, _row(r_k), _row(ln_g), _row(ln_b), batch=batch, seq=seq, tb=_tile(seq, SCAN_TILE))
    return _proj_post(y, h, bf(w_out), g_post, tm=_tile(seq, TOKEN_TILE))


def kernel(x, mem, sandwich_g, ffn_w_in, ffn_w_out, mem_norm_g, xa_w_q, xa_w_kv, xa_w_o, dn_w_in, dn_conv_w, dn_a_log, dn_dt_bias, dn_norm_g, dn_w_out, ssd_w_in, ssd_conv_w, ssd_conv_b, ssd_a_log, ssd_dt_bias, ssd_d, ssd_norm_g, ssd_w_out, rw_mu, rw_w_rkv, rw_w0, rw_w1, rw_w2, rw_a0, rw_a1, rw_a2, rw_g1, rw_g2, rw_k_k, rw_k_a, rw_r_k, rw_ln_g, rw_ln_b, rw_w_out):
    batch, seq, d = x.shape
    n_mem = mem.shape[1]
    h = x.reshape(batch * seq, d)
    mem2 = mem.reshape(batch * n_mem, d)
    tm_ffn = _tile(seq, TOKEN_TILE)

    def ffn(h, l, which, g_pre, g_post):
        return _ffn(h, _row(g_pre), _row(g_post), ffn_w_in[l, which].astype(BF16),
                    ffn_w_out[l, which].astype(BF16), tm=tm_ffn)

    for l in range(DEPTH):
        g = sandwich_g[l]
        kind = l % N_MIXERS
        j = l // N_MIXERS
        h = ffn(h, l, 0, g[0, 0], g[0, 1])
        gp, gq = _row(g[1, 0]), _row(g[1, 1])
        if kind == 0:
            h = _deltanet_layer(h, gp, gq, dn_w_in[j], dn_conv_w[j], dn_a_log[j], dn_dt_bias[j], dn_norm_g[j],
                                dn_w_out[j], batch=batch, seq=seq)
        elif kind == 1:
            h = _ssd_layer(h, gp, gq, ssd_w_in[j], ssd_conv_w[j], ssd_conv_b[j], ssd_a_log[j], ssd_dt_bias[j],
                           ssd_d[j], ssd_norm_g[j], ssd_w_out[j], batch=batch, seq=seq)
        else:
            h = _rwkv_layer(h, gp, gq, rw_mu[j], rw_w_rkv[j], rw_w0[j], rw_w1[j], rw_w2[j], rw_a0[j], rw_a1[j],
                            rw_a2[j], rw_g1[j], rw_g2[j], rw_k_k[j], rw_k_a[j], rw_r_k[j].reshape(-1),
                            rw_ln_g[j], rw_ln_b[j], rw_w_out[j], batch=batch, seq=seq)
        kv = _norm_proj(mem2, _row(mem_norm_g[l]), xa_w_kv[l].astype(BF16), tm=n_mem, out_dtype=BF16)
        h = _xattn(h, kv, _row(g[2, 0]), _row(g[2, 1]), xa_w_q[l].astype(BF16), xa_w_o[l].astype(BF16),
                   seq=seq, tm=_tile(seq, TOKEN_TILE))
        h = ffn(h, l, 1, g[3, 0], g[3, 1])
    return h.reshape(batch, seq, d)
```

```python
import functools

import jax
import jax.numpy as jnp
from jax import lax
from jax.experimental import pallas as pl
from jax.experimental.pallas import tpu as pltpu

F32 = jnp.float32
BF16 = jnp.bfloat16

D_MODEL = 1024
DEPTH = 4
N_MIXERS = 3
CHUNK = 64
CONV_W = 4
RMS_EPS = 1e-6
MACARON_W = 0.5
D_FF = 2816

DN_QK_HEADS = 8
DN_V_HEADS = 16
DN_DK = 128
DN_DV = 128
DN_Q = DN_QK_HEADS * DN_DK
DN_VAL = DN_V_HEADS * DN_DV
DN_CONV_DIM = 2 * DN_Q + DN_VAL

SSD_INNER = 2 * D_MODEL
SSD_HEADDIM = 64
SSD_HEADS = SSD_INNER // SSD_HEADDIM
SSD_GROUPS = 8
SSD_HPG = SSD_HEADS // SSD_GROUPS
SSD_STATE = 128
SSD_BC = SSD_GROUPS * SSD_STATE
SSD_CONV_DIM = SSD_INNER + 2 * SSD_BC

RW_HEAD = 64
RW_HEADS = D_MODEL // RW_HEAD
RW_GN_EPS = 64e-5

XA_HEADS = 4
XA_DH = D_MODEL // XA_HEADS

LANES = 128
SUBLANES = 8
HALO = 16
VMEM_LIMIT = 56 * 1024 * 1024
TOKEN_TILE = 512
RW_PROJ_TILE = 256
SCAN_TILE = 512


def _cparams(*sem):
    return pltpu.CompilerParams(dimension_semantics=sem, vmem_limit_bytes=VMEM_LIMIT)


def _resident(shape):
    nd = len(shape)
    return pl.BlockSpec(shape, lambda *_: (0,) * nd, pipeline_mode=pl.Buffered(1))


def _rms(x, g, eps=RMS_EPS):
    return x * lax.rsqrt(jnp.mean(x * x, axis=-1, keepdims=True) + eps) * g


def _sigmoid(x):
    return 0.5 * jnp.tanh(0.5 * x) + 0.5


def _half_silu(h):
    return h + h * jnp.tanh(h)


def _silu(x):
    return _half_silu(0.5 * x)


def _softplus(x):
    return jnp.maximum(x, 0.0) + jnp.log1p(jnp.exp(-jnp.abs(x)))


def _mm(a, b):
    return jnp.dot(a.astype(BF16), b.astype(BF16), preferred_element_type=F32)


def _mm_nt(a, b):
    return lax.dot_general(a.astype(BF16), b.astype(BF16), (((1,), (1,)), ((), ())),
                           preferred_element_type=F32)


def _tri_masks(n):
    row = lax.broadcasted_iota(jnp.int32, (n, n), 0)
    col = lax.broadcasted_iota(jnp.int32, (n, n), 1)
    return row, col


_NN = ((2,), (1,))
_NT = ((2,), (2,))
_TN = ((1,), (1,))


def _bmm(a, b, dims=_NN):
    return lax.dot_general(a.astype(BF16), b.astype(BF16), (dims, ((0,), (0,))), preferred_element_type=F32)


def _pair_masks(n):
    row = lax.broadcasted_iota(jnp.int32, (n, 2 * n), 0)
    lane = lax.broadcasted_iota(jnp.int32, (n, 2 * n), 1)
    return row, lane & (n - 1), lane // n


def _block_diag(p, blk):
    pb = p.astype(BF16)
    keep1 = blk.astype(BF16)
    return jnp.concatenate([pb * (1 - keep1), pb * keep1], axis=1)


def _unit_lower_inverse_pairs(a, row, col, blk):
    n = a.shape[1]
    eye = (row == col).astype(F32)
    x = eye - jnp.where((row >> 1) == (col >> 1), a, 0.0)
    b = 2
    while b < n:
        sh = b.bit_length() - 1
        m = ((row >> (sh + 1)) == (col >> (sh + 1))) & (((row >> sh) & 1) == 1) & (((col >> sh) & 1) == 0)
        lm = jnp.where(m, a, 0.0)
        x = x - _bmm(_bmm(x, _block_diag(lm, blk)), _block_diag(x, blk))
        b *= 2
    return x


def _cumsum_rows(lower_ones, v):
    hi = v.astype(BF16)
    r1 = v - hi.astype(F32)
    mid = r1.astype(BF16)
    lo = (r1 - mid.astype(F32)).astype(BF16)
    ones = lower_ones.astype(BF16)
    dot = lambda t: jnp.dot(ones, t, preferred_element_type=F32)
    return dot(hi) + dot(mid) + dot(lo)


def _norm_proj_kernel(x_ref, g_ref, w_ref, o_ref):
    xn = _rms(x_ref[...], g_ref[...]).astype(BF16)
    o_ref[...] = jnp.dot(xn, w_ref[...], preferred_element_type=F32).astype(o_ref.dtype)


def _norm_proj(x, g, w, *, tm, out_dtype=F32):
    t, d = x.shape
    n = w.shape[1]
    return pl.pallas_call(
        _norm_proj_kernel,
        grid=(t // tm,),
        in_specs=[pl.BlockSpec((tm, d), lambda i: (i, 0)), _resident((1, d)), _resident((d, n))],
        out_specs=pl.BlockSpec((tm, n), lambda i: (i, 0)),
        out_shape=jax.ShapeDtypeStruct((t, n), out_dtype),
        compiler_params=_cparams("parallel"),
        name="norm_proj",
    )(x, g, w)


def _norm_proj_conv_kernel(x_ref, xh_ref, g_ref, w_ref, cw_ref, cb_ref, o_ref, p_ref, *, tm, tn, tiles_per_seq):
    i = pl.program_id(0)
    g = g_ref[...]
    first = (i % tiles_per_seq) == 0
    xh = jnp.where(first, 0.0, xh_ref[...])
    xn = jnp.concatenate([_rms(xh, g), _rms(x_ref[...], g)], axis=0).astype(BF16)
    n = w_ref.shape[1]
    for c in range(n // tn):
        sl = slice(c * tn, (c + 1) * tn)
        p_ref[c % 2] = jnp.dot(xn, w_ref[:, sl], preferred_element_type=F32)
        cw = 0.5 * cw_ref[:, sl]
        hy = 0.5 * cb_ref[:, sl]
        for j in range(CONV_W):
            hy = hy + cw[j:j + 1] * p_ref[c % 2, pl.ds(HALO - (CONV_W - 1) + j, tm), :]
        o_ref[:, sl] = _half_silu(hy)


def _norm_proj_conv(x, g, w, conv_w, conv_b, *, seq, tm, tn=512):
    t, d = x.shape
    n = w.shape[1]
    hb = tm // HALO
    kern = functools.partial(_norm_proj_conv_kernel, tm=tm, tn=tn, tiles_per_seq=seq // tm)
    return pl.pallas_call(
        kern,
        grid=(t // tm,),
        in_specs=[pl.BlockSpec((tm, d), lambda i: (i, 0)),
                  pl.BlockSpec((HALO, d), lambda i: (jnp.maximum(i * hb - 1, 0), 0)),
                  _resident((1, d)), _resident((d, n)), _resident((CONV_W, n)), _resident((1, n))],
        out_specs=pl.BlockSpec((tm, n), lambda i: (i, 0)),
        out_shape=jax.ShapeDtypeStruct((t, n), F32),
        scratch_shapes=[pltpu.VMEM((2, HALO + tm, tn), F32)],
        compiler_params=_cparams("parallel"),
        name="norm_proj_conv",
    )(x, x, g, w, conv_w, conv_b)


def _proj_post_kernel(a_ref, h_ref, w_ref, g_ref, o_ref, *, weight):
    u = jnp.dot(a_ref[...], w_ref[...], preferred_element_type=F32)
    o_ref[...] = h_ref[...] + weight * _rms(u, g_ref[...])


def _proj_post(a, h, w, g_post, *, tm, weight=1.0):
    t, k = a.shape
    d = w.shape[1]
    return pl.pallas_call(
        functools.partial(_proj_post_kernel, weight=weight),
        grid=(t // tm,),
        in_specs=[pl.BlockSpec((tm, k), lambda i: (i, 0)), pl.BlockSpec((tm, d), lambda i: (i, 0)),
                  _resident((k, d)), _resident((1, d))],
        out_specs=pl.BlockSpec((tm, d), lambda i: (i, 0)),
        out_shape=jax.ShapeDtypeStruct((t, d), F32),
        compiler_params=_cparams("parallel"),
        name="proj_post",
    )(a, h, w, g_post)


def _ffn_kernel(h_ref, gpre_ref, gpost_ref, wg_ref, wu_ref, wo_ref, o_ref, acc_ref, *, tf):
    h = h_ref[...]
    xn = _rms(h, gpre_ref[...]).astype(BF16)
    nf = wg_ref.shape[1] // tf
    for c in range(nf):
        sl = slice(c * tf, (c + 1) * tf)
        gate = jnp.dot(xn, wg_ref[:, sl], preferred_element_type=F32)
        up = jnp.dot(xn, wu_ref[:, sl], preferred_element_type=F32)
        act = (_silu(gate) * up).astype(BF16)
        part = jnp.dot(act, wo_ref[sl, :], preferred_element_type=F32)
        if c == 0:
            acc_ref[...] = part
        else:
            acc_ref[...] += part
    o_ref[...] = h + MACARON_W * _rms(acc_ref[...], gpost_ref[...])


def _ffn(h, g_pre, g_post, w_in, w_out, *, tm, tf=256):
    t, d = h.shape
    f = w_out.shape[0]
    half = lambda j: pl.BlockSpec((d, f), lambda i: (0, j), pipeline_mode=pl.Buffered(1))
    return pl.pallas_call(
        functools.partial(_ffn_kernel, tf=tf),
        grid=(t // tm,),
        in_specs=[pl.BlockSpec((tm, d), lambda i: (i, 0)), _resident((1, d)), _resident((1, d)),
                  half(0), half(1), _resident((f, d))],
        out_specs=pl.BlockSpec((tm, d), lambda i: (i, 0)),
        out_shape=jax.ShapeDtypeStruct((t, d), F32),
        scratch_shapes=[pltpu.VMEM((tm, d), F32)],
        compiler_params=_cparams("parallel"),
        name="ffn",
    )(h, g_pre, g_post, w_in, w_in, w_out)


def _xattn_kernel(h_ref, kv_ref, gpre_ref, gpost_ref, wq_ref, wo_ref, o_ref):
    h = h_ref[...]
    d = h.shape[1]
    xn = _rms(h, gpre_ref[...]).astype(BF16)
    q = jnp.dot(xn, wq_ref[...], preferred_element_type=F32) * XA_DH ** -0.5
    outs = []
    for hh in range(XA_HEADS):
        sl = slice(hh * XA_DH, (hh + 1) * XA_DH)
        k = kv_ref[:, sl]
        v = kv_ref[:, d + hh * XA_DH:d + (hh + 1) * XA_DH]
        s = _mm_nt(q[:, sl], k)
        s = s - jnp.max(s, axis=-1, keepdims=True)
        e = jnp.exp(s)
        p = e / jnp.sum(e, axis=-1, keepdims=True)
        outs.append(_mm(p, v))
    o = jnp.concatenate(outs, axis=-1).astype(BF16)
    u = jnp.dot(o, wo_ref[...], preferred_element_type=F32)
    o_ref[...] = h + _rms(u, gpost_ref[...])


def _xattn(h, kv, g_pre, g_post, w_q, w_o, *, seq, tm):
    t, d = h.shape
    n_mem = kv.shape[0] // (t // seq)
    tiles_per_seq = seq // tm
    return pl.pallas_call(
        _xattn_kernel,
        grid=(t // tm,),
        in_specs=[pl.BlockSpec((tm, d), lambda i: (i, 0)),
                  pl.BlockSpec((n_mem, 2 * d), lambda i: (i // tiles_per_seq, 0)),
                  _resident((1, d)), _resident((1, d)), _resident((d, d)), _resident((d, d))],
        out_specs=pl.BlockSpec((tm, d), lambda i: (i, 0)),
        out_shape=jax.ShapeDtypeStruct((t, d), F32),
        compiler_params=_cparams("parallel"),
        name="xattn",
    )(h, kv, g_pre, g_post, w_q, w_o)


def _dn_scan_kernel(qkv_ref, zb_ref, alog_ref, dtb_ref, ng_ref, o_ref, s_ref, *, tb):
    @pl.when(pl.program_id(1) == 0)
    def _():
        s_ref[...] = jnp.zeros_like(s_ref)

    row, col = _tri_masks(CHUNK)
    lower_ones = (col <= row).astype(F32)
    rowp, colp, blk = _pair_masks(CHUNK)
    incl_p = colp <= rowp
    strict_p = colp < rowp
    rep = DN_V_HEADS // DN_QK_HEADS
    assert rep == 2
    neg_a = -jnp.exp(alog_ref[...])
    dtb = dtb_ref[...]
    ng = ng_ref[...]

    def chunk(c, carry):
        r0 = pl.multiple_of(c * CHUNK, CHUNK)
        rows = pl.ds(r0, CHUNK)
        ba = zb_ref[rows, DN_VAL:DN_VAL + LANES]
        beta_all = _sigmoid(ba)
        g_all = neg_a * _softplus(ba + dtb)
        gc = _cumsum_rows(lower_ones, g_all)
        gct = gc.T
        eg_all = jnp.exp(gc)
        glast = gc[CHUNK - 1:CHUNK, :]
        eend_all = jnp.exp(glast - gc)
        elast_all = jnp.exp(glast)
        qs, ks = [], []
        for hq in range(DN_QK_HEADS):
            q = qkv_ref[rows, hq * DN_DK:(hq + 1) * DN_DK]
            k = qkv_ref[rows, DN_Q + hq * DN_DK:DN_Q + (hq + 1) * DN_DK]
            qs.append(q * lax.rsqrt(jnp.sum(q * q, axis=-1, keepdims=True) + 1e-6) * DN_DK ** -0.5)
            ks.append(k * lax.rsqrt(jnp.sum(k * k, axis=-1, keepdims=True) + 1e-6))
        k8 = jnp.stack(ks)
        qkk = _bmm(jnp.stack([jnp.concatenate([q, k], axis=0) for q, k in zip(qs, ks)]),
                   jnp.concatenate([k8, k8], axis=1), _NT)
        qk_p, kk_p = qkk[:, :CHUNK], qkk[:, CHUNK:]
        a_l, att_l = [], []
        for hq in range(DN_QK_HEADS):
            h0 = hq * rep
            beta_p = jnp.take_along_axis(beta_all, blk + h0, axis=1)
            g_col = jnp.take_along_axis(gc, blk + (DN_V_HEADS + h0), axis=1)
            g_row = jnp.concatenate([gct[DN_V_HEADS + h0 + e:DN_V_HEADS + h0 + e + 1, :] for e in range(rep)], axis=1)
            decay = jnp.where(incl_p, jnp.exp(g_col - g_row), 0.0)
            a_l.append(jnp.where(strict_p, kk_p[hq] * decay * beta_p, 0.0))
            att_l.append(qk_p[hq] * decay)
        t_inv = _unit_lower_inverse_pairs(jnp.stack(a_l), rowp, colp, blk)
        t_l, rhs_l, qe_l, kend_l, elast_l = [], [], [], [], []
        for h in range(DN_V_HEADS):
            hq, e = divmod(h, rep)
            gl = DN_V_HEADS + h
            beta = beta_all[:, h:h + 1]
            eg = eg_all[:, gl:gl + 1]
            v = qkv_ref[rows, 2 * DN_Q + h * DN_DV:2 * DN_Q + (h + 1) * DN_DV]
            t_l.append(t_inv[hq][:, e * CHUNK:(e + 1) * CHUNK])
            rhs_l.append(jnp.concatenate([v * beta, ks[hq] * (beta * eg)], axis=-1))
            qe_l.append(qs[hq] * eg)
            kend_l.append(ks[hq] * eend_all[:, gl:gl + 1])
            elast_l.append(elast_all[:, gl:gl + 1])
        sol = _bmm(jnp.stack(t_l), jnp.stack(rhs_l))
        u, wk = sol[:, :, :DN_DV], sol[:, :, DN_DV:]
        state = s_ref[...]
        ws = _bmm(jnp.concatenate([wk, jnp.stack(qe_l)], axis=1), state)
        v_new = u - ws[:, :CHUNK]
        zero = jnp.zeros((CHUNK, DN_DV), F32)
        vn_bd = jnp.stack([jnp.concatenate([jnp.concatenate([v_new[hq * rep], zero], axis=1),
                                            jnp.concatenate([zero, v_new[hq * rep + 1]], axis=1)], axis=0)
                           for hq in range(DN_QK_HEADS)])
        o_p = _bmm(jnp.stack(att_l), vn_bd)
        s_ref[...] = state * jnp.stack(elast_l) + _bmm(jnp.stack(kend_l), v_new, _TN)
        for h in range(DN_V_HEADS):
            hq, e = divmod(h, rep)
            vsl = slice(h * DN_DV, (h + 1) * DN_DV)
            o = ws[h, CHUNK:] + o_p[hq][:, e * DN_DV:(e + 1) * DN_DV]
            z = zb_ref[rows, vsl]
            o_ref[rows, vsl] = (_rms(o, ng) * _silu(z)).astype(o_ref.dtype)
        return carry

    lax.fori_loop(0, tb // CHUNK, chunk, 0, unroll=2)


def _dn_scan(qkv, zb, a_log_row, dt_bias_row, norm_g, *, batch, seq, tb):
    t = qkv.shape[0]
    nt = seq // tb
    return pl.pallas_call(
        functools.partial(_dn_scan_kernel, tb=tb),
        grid=(batch, nt),
        in_specs=[pl.BlockSpec((tb, DN_CONV_DIM), lambda b, i: (b * nt + i, 0)),
                  pl.BlockSpec((tb, DN_VAL + LANES), lambda b, i: (b * nt + i, 0)),
                  _resident((1, LANES)), _resident((1, LANES)), _resident((1, DN_DV))],
        out_specs=pl.BlockSpec((tb, DN_VAL), lambda b, i: (b * nt + i, 0)),
        out_shape=jax.ShapeDtypeStruct((t, DN_VAL), BF16),
        scratch_shapes=[pltpu.VMEM((DN_V_HEADS, DN_DK, DN_DV), F32)],
        compiler_params=_cparams("parallel", "arbitrary"),
        name="dn_scan",
    )(qkv, zb, a_log_row, dt_bias_row, norm_g)


def _ssd_scan_kernel(xbc_ref, zd_ref, alog_ref, dtb_ref, dskip_ref, ng_ref, o_ref, s_ref, *, tb):
    @pl.when(pl.program_id(1) == 0)
    def _():
        s_ref[...] = jnp.zeros_like(s_ref)

    row, col = _tri_masks(CHUNK)
    lower_ones = (col <= row).astype(F32)
    neg_a = -jnp.exp(alog_ref[...])
    dtb = dtb_ref[...]
    gsz = SSD_HPG * SSD_HEADDIM
    lane = lax.broadcasted_iota(jnp.int32, (CHUNK, LANES), 1)
    pair = lane // SSD_HEADDIM
    rowg = lax.broadcasted_iota(jnp.int32, (CHUNK, gsz), 0)
    laneg = lax.broadcasted_iota(jnp.int32, (CHUNK, gsz), 1)
    incl_g = (laneg & (SSD_HEADDIM - 1)) <= rowg
    head_g = laneg // SSD_HEADDIM

    def spread(cols, g):
        halves = [jnp.take_along_axis(cols, pair + (g * SSD_HPG + 2 * p), axis=1) for p in range(2)]
        return jnp.concatenate(halves, axis=1)

    def chunk(c, carry):
        r0 = pl.multiple_of(c * CHUNK, CHUNK)
        rows = pl.ds(r0, CHUNK)
        dt_all = _softplus(zd_ref[rows, SSD_INNER:SSD_INNER + LANES] + dtb)
        ac = _cumsum_rows(lower_ones, dt_all * neg_a)
        act = ac.T
        bm = jnp.stack([xbc_ref[rows, SSD_INNER + g * SSD_STATE:SSD_INNER + (g + 1) * SSD_STATE]
                        for g in range(SSD_GROUPS)])
        cm = jnp.stack([xbc_ref[rows, SSD_INNER + SSD_BC + g * SSD_STATE:SSD_INNER + SSD_BC + (g + 1) * SSD_STATE]
                        for g in range(SSD_GROUPS)])
        cb = _bmm(cm, jnp.concatenate([bm] * SSD_HPG, axis=1), _NT)
        state = s_ref[...]
        y_state = _bmm(cm, state)
        w_l, blk_l, xe_l, el_l, ea_l = [], [], [], [], []
        for g in range(SSD_GROUPS):
            xg = xbc_ref[rows, g * gsz:(g + 1) * gsz]
            ac_col = spread(ac, g)
            ac_row = jnp.concatenate([act[g * SSD_HPG + e:g * SSD_HPG + e + 1, :] for e in range(SSD_HPG)], axis=1)
            a_last = ac_col[CHUNK - 1:CHUNK, :]
            xdt = xg * spread(dt_all, g)
            w_l.append(cb[g] * jnp.where(incl_g, jnp.exp(ac_col - ac_row), 0.0))
            blk_l.append(jnp.concatenate([jnp.where(head_g == e, xdt, 0.0) for e in range(SSD_HPG)], axis=0))
            xe_l.append(xdt * jnp.exp(a_last - ac_col))
            el_l.append(jnp.exp(a_last))
            ea_l.append(jnp.exp(ac_col))
        y_intra = _bmm(jnp.stack(w_l), jnp.stack(blk_l))
        s_ref[...] = state * jnp.stack(el_l) + _bmm(bm, jnp.stack(xe_l), _TN)
        for g in range(SSD_GROUPS):
            gsl = slice(g * gsz, (g + 1) * gsz)
            y = y_intra[g] + y_state[g] * ea_l[g]
            y = y + xbc_ref[rows, gsl] * dskip_ref[:, gsl]
            yz = y * _silu(zd_ref[rows, gsl])
            o_ref[rows, gsl] = _rms(yz, ng_ref[:, gsl]).astype(o_ref.dtype)
        return carry

    lax.fori_loop(0, tb // CHUNK, chunk, 0, unroll=4)


def _ssd_scan(xbc, zd, a_log_row, dt_bias_row, d_skip_row, norm_g, *, batch, seq, tb):
    t = xbc.shape[0]
    nt = seq // tb
    return pl.pallas_call(
        functools.partial(_ssd_scan_kernel, tb=tb),
        grid=(batch, nt),
        in_specs=[pl.BlockSpec((tb, SSD_CONV_DIM), lambda b, i: (b * nt + i, 0)),
                  pl.BlockSpec((tb, SSD_INNER + LANES), lambda b, i: (b * nt + i, 0)),
                  _resident((1, LANES)), _resident((1, LANES)),
                  _resident((1, SSD_INNER)), _resident((1, SSD_INNER))],
        out_specs=pl.BlockSpec((tb, SSD_INNER), lambda b, i: (b * nt + i, 0)),
        out_shape=jax.ShapeDtypeStruct((t, SSD_INNER), BF16),
        scratch_shapes=[pltpu.VMEM((SSD_GROUPS, SSD_STATE, SSD_HPG * SSD_HEADDIM), F32)],
        compiler_params=_cparams("parallel", "arbitrary"),
        name="ssd_scan",
    )(xbc, zd, a_log_row, dt_bias_row, d_skip_row, norm_g)


def _rw_proj_kernel(x_ref, xh_ref, gpre_ref, mu_ref, wrkv_ref, w0_ref, w1_ref, w2_ref, a0_ref, a1_ref, a2_ref,
                    g1_ref, g2_ref, kk_ref, ka_ref,
                    r_out, lw_out, k_out, v_out, kk_out, a_out, gate_out, *, tm, tiles_per_seq):
    i = pl.program_id(0)
    g = gpre_ref[...]
    first = (i % tiles_per_seq) == 0
    x = _rms(x_ref[...], g)
    xh = _rms(jnp.where(first, 0.0, xh_ref[...]), g)
    prev = jnp.concatenate([xh, x], axis=0)[HALO - 1:HALO - 1 + tm]
    xx = prev - x
    mu = mu_ref[...]
    xr, xw, xk, xv, xa, xg = [(x + xx * mu[j:j + 1]) for j in range(6)]
    r = _mm(xr, wrkv_ref[0])
    k = _mm(xk, wrkv_ref[1])
    v = _mm(xv, wrkv_ref[2])
    w = -_softplus(-(w0_ref[...] + _mm(jnp.tanh(_mm(xw, w1_ref[...])), w2_ref[...]))) - 0.5
    a = _sigmoid(a0_ref[...] + _mm(_mm(xa, a1_ref[...]), a2_ref[...]))
    gate = _mm(_sigmoid(_mm(xg, g1_ref[...])), g2_ref[...])
    r_out[...] = r.astype(r_out.dtype)
    lw_out[...] = -jnp.exp(w)
    k_out[...] = (k * (1.0 + (a - 1.0) * ka_ref[...])).astype(k_out.dtype)
    v_out[...] = v.astype(v_out.dtype)
    kk_out[...] = (k * kk_ref[...]).astype(kk_out.dtype)
    a_out[...] = a.astype(a_out.dtype)
    gate_out[...] = gate.astype(gate_out.dtype)


def _rw_proj(h, g_pre, mu, w_rkv, w0, w1, w2, a0, a1, a2, g1, g2, k_k, k_a, *, seq, tm):
    t, d = h.shape
    hb = tm // HALO
    row = lambda: _resident((1, d))
    tok = pl.BlockSpec((tm, d), lambda i: (i, 0))
    outs = [jax.ShapeDtypeStruct((t, d), F32 if j == 1 else BF16) for j in range(7)]
    return pl.pallas_call(
        functools.partial(_rw_proj_kernel, tm=tm, tiles_per_seq=seq // tm),
        grid=(t // tm,),
        in_specs=[tok, pl.BlockSpec((HALO, d), lambda i: (jnp.maximum(i * hb - 1, 0), 0)),
                  row(), _resident(mu.shape), _resident(w_rkv.shape), row(), _resident(w1.shape),
                  _resident(w2.shape), row(), _resident(a1.shape), _resident(a2.shape),
                  _resident(g1.shape), _resident(g2.shape), row(), row()],
        out_specs=[tok] * 7,
        out_shape=outs,
        compiler_params=_cparams("parallel"),
        name="rw_proj",
    )(h, h, g_pre, mu, w_rkv, w0, w1, w2, a0, a1, a2, g1, g2, k_k, k_a)


def _rw_scan_kernel(r_ref, lw_ref, k_ref, v_ref, kk_ref, a_ref, gate_ref, rk_ref, lng_ref, lnb_ref,
                    o_ref, s_ref, *, tb):
    @pl.when(pl.program_id(1) == 0)
    def _():
        s_ref[...] = jnp.zeros_like(s_ref)

    row, col = _tri_masks(CHUNK)
    lower_ones = (col <= row).astype(F32)
    rowp, colp, blk = _pair_masks(CHUNK)
    assert RW_HEAD == CHUNK and 2 * RW_HEAD == LANES
    incl_p = colp <= rowp
    strict_p = colp < rowp
    first = blk == 0
    keep1 = blk[:1].astype(F32)
    keep0 = 1.0 - keep1
    npairs = RW_HEADS // 2

    def pairs(t):
        return jnp.stack([t[:, p * LANES:(p + 1) * LANES] for p in range(npairs)])

    def flat(t):
        return jnp.concatenate([t[p] for p in range(npairs)], axis=-1)

    def head_sum(t):
        s0 = jnp.sum(jnp.where(first, t, 0.0), axis=-1, keepdims=True)
        s1 = jnp.sum(jnp.where(first, 0.0, t), axis=-1, keepdims=True)
        return jnp.where(first, s0, s1)

    def to_pairs(rows0, rows1):
        bk0, bk1 = rows0[:, :, :LANES], rows1[:, :, :LANES]
        m_b = jnp.where(first, bk0, pltpu.roll(bk1, RW_HEAD, axis=2))
        m_k = jnp.where(first, pltpu.roll(bk0, RW_HEAD, axis=2), bk1)
        m_s = jnp.where(first, rows0[:, :, LANES:], rows1[:, :, LANES:])
        return m_b, m_k, m_s

    def chunk(c, carry):
        r0 = pl.multiple_of(c * CHUNK, CHUNK)
        rows = pl.ds(r0, CHUNK)
        lw = lw_ref[rows, :]
        wc = _cumsum_rows(lower_ones, lw)
        e_inv = jnp.exp(-wc)
        r = r_ref[rows, :].astype(F32)
        k = k_ref[rows, :].astype(F32)
        vp = pairs(v_ref[rows, :].astype(F32))
        kk = pairs(kk_ref[rows, :].astype(F32))
        kk = kk * lax.rsqrt(head_sum(kk * kk) + 1e-6)
        rt = pairs(r * jnp.exp(wc))
        kt = pairs(k * e_inv)
        at = -kk * pairs(jnp.exp(wc - lw))
        bt = kk * pairs(a_ref[rows, :].astype(F32) * e_inv)
        state = s_ref[...]
        ar = jnp.concatenate([at, rt], axis=1)
        p = _bmm(jnp.concatenate([ar * keep0, ar * keep1], axis=1),
                 jnp.concatenate([bt, kt, state, state], axis=1), _NT)
        a_ab, a_ak, a_s = to_pairs(p[:, :CHUNK], p[:, 2 * CHUNK:3 * CHUNK])
        r_b, r_k, r_s = to_pairs(p[:, CHUNK:2 * CHUNK], p[:, 3 * CHUNK:])
        t_inv = _unit_lower_inverse_pairs(jnp.where(strict_p, -a_ab, 0.0), rowp, colp, blk)
        v_bd = _block_diag(vp, blk)
        u = _bmm(t_inv, _block_diag(a_s + _bmm(jnp.where(strict_p, a_ak, 0.0), v_bd), blk))
        m = jnp.concatenate([jnp.where(incl_p, r_b, 0.0), jnp.where(incl_p, r_k, 0.0)], axis=2)
        y = r_s + _bmm(m, jnp.concatenate([_block_diag(u, blk), v_bd], axis=1))
        prod = _bmm(jnp.concatenate([u, vp], axis=1), jnp.concatenate([bt, kt], axis=1), _TN)
        upd = jnp.where(first, prod[:, :RW_HEAD], prod[:, RW_HEAD:])
        s_ref[...] = (state + upd) * pairs(jnp.exp(wc[CHUNK - 1:CHUNK, :]))
        yc = y - head_sum(y) * (1.0 / RW_HEAD)
        yn = yc * lax.rsqrt(head_sum(yc * yc) * (1.0 / RW_HEAD) + RW_GN_EPS)
        bonus = head_sum(pairs(r * k * rk_ref[...])) * vp
        out = (flat(yn) * lng_ref[...] + lnb_ref[...] + flat(bonus)) * gate_ref[rows, :].astype(F32)
        o_ref[rows, :] = out.astype(o_ref.dtype)
        return carry

    lax.fori_loop(0, tb // CHUNK, chunk, 0, unroll=2)


def _rw_scan(r, lw, k, v, kk, a, gate, r_k, ln_g, ln_b, *, batch, seq, tb):
    t, d = r.shape
    nt = seq // tb
    tok = pl.BlockSpec((tb, d), lambda bi, i: (bi * nt + i, 0))
    return pl.pallas_call(
        functools.partial(_rw_scan_kernel, tb=tb),
        grid=(batch, nt),
        in_specs=[tok] * 7 + [_resident((1, d))] * 3,
        out_specs=tok,
        out_shape=jax.ShapeDtypeStruct((t, d), BF16),
        scratch_shapes=[pltpu.VMEM((RW_HEADS // 2, RW_HEAD, 2 * RW_HEAD), F32)],
        compiler_params=_cparams("parallel", "arbitrary"),
        name="rw_scan",
    )(r, lw, k, v, kk, a, gate, r_k, ln_g, ln_b)


def _row(v):
    return v.reshape(1, -1).astype(F32)


def _lane_row(v, offset):
    return jnp.zeros((1, LANES), F32).at[0, offset:offset + v.shape[0]].set(v.astype(F32))


def _pad_cols(w, n):
    return jnp.pad(w, ((0, 0), (0, n - w.shape[1])))


def _tile(seq, want):
    tile = min(seq, want)
    assert seq % tile == 0 and tile % CHUNK == 0, (seq, tile)
    return tile


def _deltanet_layer(h, g_pre, g_post, w_in, conv_w, a_log, dt_bias, norm_g, w_out, *, batch, seq):
    w_conv = w_in[:, :DN_CONV_DIM].astype(BF16)
    w_rest = _pad_cols(w_in[:, DN_CONV_DIM:], DN_VAL + LANES).astype(BF16)
    qkv = _norm_proj_conv(h, g_pre, w_conv, conv_w, jnp.zeros((1, DN_CONV_DIM), F32),
                          seq=seq, tm=_tile(seq, TOKEN_TILE))
    zb = _norm_proj(h, g_pre, w_rest, tm=_tile(seq, TOKEN_TILE))
    o = _dn_scan(qkv, zb, _lane_row(a_log, DN_V_HEADS), _lane_row(dt_bias, DN_V_HEADS), _row(norm_g),
                 batch=batch, seq=seq, tb=_tile(seq, SCAN_TILE))
    return _proj_post(o, h, w_out.astype(BF16), g_post, tm=_tile(seq, TOKEN_TILE))


def _ssd_layer(h, g_pre, g_post, w_in, conv_w, conv_b, a_log, dt_bias, d_skip, norm_g, w_out, *, batch, seq):
    w_conv = w_in[:, SSD_INNER:SSD_INNER + SSD_CONV_DIM].astype(BF16)
    w_rest = _pad_cols(jnp.concatenate([w_in[:, :SSD_INNER], w_in[:, SSD_INNER + SSD_CONV_DIM:]], axis=1),
                       SSD_INNER + LANES).astype(BF16)
    xbc = _norm_proj_conv(h, g_pre, w_conv, conv_w, _row(conv_b), seq=seq, tm=_tile(seq, TOKEN_TILE))
    zd = _norm_proj(h, g_pre, w_rest, tm=_tile(seq, TOKEN_TILE))
    y = _ssd_scan(xbc, zd, _lane_row(a_log, 0), _lane_row(dt_bias, 0),
                  _row(jnp.repeat(d_skip, SSD_HEADDIM)), _row(norm_g), batch=batch, seq=seq, tb=_tile(seq, SCAN_TILE))
    return _proj_post(y, h, w_out.astype(BF16), g_post, tm=_tile(seq, TOKEN_TILE))


def _rwkv_layer(h, g_pre, g_post, mu, w_rkv, w0, w1, w2, a0, a1, a2, g1, g2, k_k, k_a, r_k, ln_g, ln_b, w_out,
                *, batch, seq):
    bf = lambda w: w.astype(BF16)
    outs = _rw_proj(h, g_pre, mu.astype(F32), bf(w_rkv), _row(w0), bf(w1), bf(w2), _row(a0), bf(a1), bf(a2),
                    bf(g1), bf(g2), _row(k_k), _row(k_a), seq=seq, tm=_tile(seq, RW_PROJ_TILE))
    y = _rw_scan(*outs, _row(r_k), _row(ln_g), _row(ln_b), batch=batch, seq=seq, tb=_tile(seq, SCAN_TILE))
    return _proj_post(y, h, bf(w_out), g_post, tm=_tile(seq, TOKEN_TILE))


def kernel(x, mem, sandwich_g, ffn_w_in, ffn_w_out, mem_norm_g, xa_w_q, xa_w_kv, xa_w_o, dn_w_in, dn_conv_w, dn_a_log, dn_dt_bias, dn_norm_g, dn_w_out, ssd_w_in, ssd_conv_w, ssd_conv_b, ssd_a_log, ssd_dt_bias, ssd_d, ssd_norm_g, ssd_w_out, rw_mu, rw_w_rkv, rw_w0, rw_w1, rw_w2, rw_a0, rw_a1, rw_a2, rw_g1, rw_g2, rw_k_k, rw_k_a, rw_r_k, rw_ln_g, rw_ln_b, rw_w_out):
    batch, seq, d = x.shape
    n_mem = mem.shape[1]
    h = x.reshape(batch * seq, d)
    mem2 = mem.reshape(batch * n_mem, d)
    tm_ffn = _tile(seq, TOKEN_TILE)

    def ffn(h, l, which, g_pre, g_post):
        return _ffn(h, _row(g_pre), _row(g_post), ffn_w_in[l, which].astype(BF16),
                    ffn_w_out[l, which].astype(BF16), tm=tm_ffn)

    for l in range(DEPTH):
        g = sandwich_g[l]
        kind = l % N_MIXERS
        j = l // N_MIXERS
        h = ffn(h, l, 0, g[0, 0], g[0, 1])
        gp, gq = _row(g[1, 0]), _row(g[1, 1])
        if kind == 0:
            h = _deltanet_layer(h, gp, gq, dn_w_in[j], dn_conv_w[j], dn_a_log[j], dn_dt_bias[j], dn_norm_g[j],
                                dn_w_out[j], batch=batch, seq=seq)
        elif kind == 1:
            h = _ssd_layer(h, gp, gq, ssd_w_in[j], ssd_conv_w[j], ssd_conv_b[j], ssd_a_log[j], ssd_dt_bias[j],
                           ssd_d[j], ssd_norm_g[j], ssd_w_out[j], batch=batch, seq=seq)
        else:
            h = _rwkv_layer(h, gp, gq, rw_mu[j], rw_w_rkv[j], rw_w0[j], rw_w1[j], rw_w2[j], rw_a0[j], rw_a1[j],
                            rw_a2[j], rw_g1[j], rw_g2[j], rw_k_k[j], rw_k_a[j], rw_r_k[j].reshape(-1),
                            rw_ln_g[j], rw_ln_b[j], rw_w_out[j], batch=batch, seq=seq)
        kv = _norm_proj(mem2, _row(mem_norm_g[l]), xa_w_kv[l].astype(BF16), tm=n_mem, out_dtype=BF16)
        h = _xattn(h, kv, _row(g[2, 0]), _row(g[2, 1]), xa_w_q[l].astype(BF16), xa_w_o[l].astype(BF16),
                   seq=seq, tm=_tile(seq, TOKEN_TILE))
        h = ffn(h, l, 1, g[3, 0], g[3, 1])
    return h.reshape(batch, seq, d)
```
